```python
import math
import jax, jax.numpy as jnp
from jax import lax
import numpy as np

D_MODEL = 2048
BATCH = 1
SEQ = 8192
DEPTH = 4

CHUNK = 64
Q_BLOCK = 128
N_MIXERS = 2
MLA_HEADS = 16
Q_LORA = 512
KV_LORA = 512
NOPE_DIM = 128
ROPE_DIM = 64
V_DIM = 128
ROPE_THETA = 10000.0
SB_HEADS = 16
SB_HEAD_DIM = D_MODEL // SB_HEADS
N_MEM = 256
MEM_HEADS = 4
MEM_HEAD_DIM = 128
D_FF = 5632
CONV_W = 3
EPS = 1e-6

N_A = (DEPTH + 1) // 2
N_B = DEPTH // 2

kernel_name = "hybrid_mla_stickbreaking_convffn_trunk"


def _rmsnorm(x, g):
    xf = x.astype(jnp.float32)
    y = xf * lax.rsqrt(jnp.mean(xf * xf, axis=-1, keepdims=True) + EPS)
    return (y * g.astype(jnp.float32)).astype(x.dtype)


def _rope_tables(positions, dtype):
    inv_freq = ROPE_THETA ** (-jnp.arange(0, ROPE_DIM, 2, dtype=jnp.float32) / ROPE_DIM)
    ang = positions.astype(jnp.float32)[..., None] * inv_freq
    ang = jnp.concatenate([ang, ang], axis=-1)
    return jnp.cos(ang).astype(dtype), jnp.sin(ang).astype(dtype)


def _apply_rope(x, cos, sin):
    half = ROPE_DIM // 2
    rot = jnp.concatenate([-x[..., half:], x[..., :half]], axis=-1)
    return x * cos + rot * sin


def _sweep_query_blocks(block_fn, qs):
    b, h, s, _ = qs[0].shape
    nb = s // Q_BLOCK
    qb = tuple(q.reshape(b, h, nb, Q_BLOCK, q.shape[-1]).transpose(2, 0, 1, 3, 4) for q in qs)
    starts = jnp.arange(nb, dtype=jnp.int32) * Q_BLOCK
    out = lax.map(lambda a: block_fn(*a), (*qb, starts))
    return out.transpose(1, 2, 0, 3, 4).reshape(b, h, s, out.shape[-1])


def _mla(h, cos, sin, w_down, q_norm, w_uq, kv_norm, w_ukv, w_o):
    b, s, _ = h.shape
    down = h @ w_down
    c_q = _rmsnorm(down[..., :Q_LORA], q_norm)
    c_kv = _rmsnorm(down[..., Q_LORA:Q_LORA + KV_LORA], kv_norm)
    k_rope = _apply_rope(down[..., Q_LORA + KV_LORA:], cos, sin)
    q = (c_q @ w_uq).reshape(b, s, MLA_HEADS, NOPE_DIM + ROPE_DIM)
    q_nope = q[..., :NOPE_DIM].transpose(0, 2, 1, 3)
    q_rope = _apply_rope(q[..., NOPE_DIM:], cos[:, :, None], sin[:, :, None]).transpose(0, 2, 1, 3)
    kv = (c_kv @ w_ukv).reshape(b, s, MLA_HEADS, NOPE_DIM + V_DIM)
    k_nope = kv[..., :NOPE_DIM].transpose(0, 2, 1, 3)
    v = kv[..., NOPE_DIM:].transpose(0, 2, 1, 3)
    scale = 1.0 / math.sqrt(NOPE_DIM + ROPE_DIM)
    k_chunk = jnp.arange(s, dtype=jnp.int32) // CHUNK

    def block(qn, qr, start):
        sc = (jnp.einsum('bhqd,bhkd->bhqk', qn, k_nope)
              + jnp.einsum('bhqd,bkd->bhqk', qr, k_rope)).astype(jnp.float32) * scale
        q_chunk = (start + jnp.arange(Q_BLOCK, dtype=jnp.int32)) // CHUNK
        allowed = k_chunk[None, :] <= q_chunk[:, None]
        p = jax.nn.softmax(jnp.where(allowed, sc, -jnp.inf), axis=-1).astype(v.dtype)
        return jnp.einsum('bhqk,bhkd->bhqd', p, v)

    o = _sweep_query_blocks(block, (q_nope, q_rope))
    return o.transpose(0, 2, 1, 3).reshape(b, s, MLA_HEADS * V_DIM) @ w_o


def _stick_breaking(h, w_qkv, w_o):
    b, s, _ = h.shape
    qkv = (h @ w_qkv).reshape(b, s, 3, SB_HEADS, SB_HEAD_DIM).transpose(2, 0, 3, 1, 4)
    q, k, v = qkv[0], qkv[1], qkv[2]
    scale = 1.0 / math.sqrt(SB_HEAD_DIM)
    k_idx = jnp.arange(s, dtype=jnp.int32)

    def block(qb, start):
        z = jnp.einsum('bhqd,bhkd->bhqk', qb, k).astype(jnp.float32) * scale
        t_idx = start + jnp.arange(Q_BLOCK, dtype=jnp.int32)
        strict = k_idx[None, :] < t_idx[:, None]
        log_keep = jnp.where(strict, jax.nn.log_sigmoid(-z), 0.0)
        later = lax.cumsum(log_keep, axis=3, reverse=True) - log_keep
        a = jnp.where(strict, jnp.exp(jax.nn.log_sigmoid(z) + later), 0.0).astype(v.dtype)
        return jnp.einsum('bhqk,bhkd->bhqd', a, v)

    o = _sweep_query_blocks(block, (q,))
    return o.transpose(0, 2, 1, 3).reshape(b, s, SB_HEADS * SB_HEAD_DIM) @ w_o


def _mem_cross_attn(h, hm, w_q, w_kv, w_o):
    b, s, _ = h.shape
    q = (h @ w_q).reshape(b, s, MEM_HEADS, MEM_HEAD_DIM)
    kv = (hm @ w_kv).reshape(b, hm.shape[1], 2, MEM_HEADS, MEM_HEAD_DIM)
    sc = jnp.einsum('bqhd,bkhd->bhqk', q, kv[:, :, 0]).astype(jnp.float32) / math.sqrt(MEM_HEAD_DIM)
    p = jax.nn.softmax(sc, axis=-1).astype(h.dtype)
    o = jnp.einsum('bhqk,bkhd->bqhd', p, kv[:, :, 1])
    return o.reshape(b, s, MEM_HEADS * MEM_HEAD_DIM) @ w_o


def _conv_ffn(h, w_in, conv_w, conv_b, w_out):
    s = h.shape[1]
    u = h @ w_in
    pad = jnp.pad(u, ((0, 0), (CONV_W - 1, 0), (0, 0)))
    u = conv_b + sum(conv_w[j] * pad[:, j:j + s] for j in range(CONV_W))
    gate, up = u[..., :D_FF], u[..., D_FF:]
    return (jax.nn.silu(gate) * up) @ w_out


def setup_inputs(seed: int = 0) -> dict:
    key = jax.random.key(seed)
    ks = jax.random.split(key, 24)
    f32 = jnp.float32

    def w(k, shape, fan_in):
        return jax.random.normal(k, shape, f32) * (fan_in ** -0.5)

    def gain(k, shape):
        return 1.0 + 0.01 * jax.random.normal(k, shape, f32)

    D = D_MODEL
    return {
        "x": jax.random.normal(ks[0], (BATCH, SEQ, D), f32),
        "mem": jax.random.normal(ks[1], (BATCH, N_MEM, D), f32),
        "positions": jnp.broadcast_to(jnp.arange(SEQ, dtype=jnp.int32), (BATCH, SEQ)),
        "norm_mix": gain(ks[2], (DEPTH, D)),
        "norm_mem_q": gain(ks[3], (DEPTH, D)),
        "norm_mem_kv": gain(ks[4], (DEPTH, D)),
        "norm_ffn": gain(ks[5], (DEPTH, D)),
        "norm_final": gain(ks[6], (D,)),
        "mla_w_down": w(ks[7], (N_A, D, Q_LORA + KV_LORA + ROPE_DIM), D),
        "mla_q_norm": gain(ks[8], (N_A, Q_LORA)),
        "mla_w_uq": w(ks[9], (N_A, Q_LORA, MLA_HEADS * (NOPE_DIM + ROPE_DIM)), Q_LORA),
        "mla_kv_norm": gain(ks[10], (N_A, KV_LORA)),
        "mla_w_ukv": w(ks[11], (N_A, KV_LORA, MLA_HEADS * (NOPE_DIM + V_DIM)), KV_LORA),
        "mla_w_o": w(ks[12], (N_A, MLA_HEADS * V_DIM, D), MLA_HEADS * V_DIM),
        "sb_w_qkv": w(ks[13], (N_B, D, 3 * SB_HEADS * SB_HEAD_DIM), D),
        "sb_w_o": w(ks[14], (N_B, SB_HEADS * SB_HEAD_DIM, D), SB_HEADS * SB_HEAD_DIM),
        "mem_w_q": w(ks[15], (DEPTH, D, MEM_HEADS * MEM_HEAD_DIM), D),
        "mem_w_kv": w(ks[16], (DEPTH, D, 2 * MEM_HEADS * MEM_HEAD_DIM), D),
        "mem_w_o": w(ks[17], (DEPTH, MEM_HEADS * MEM_HEAD_DIM, D), MEM_HEADS * MEM_HEAD_DIM),
        "ffn_w_in": w(ks[18], (DEPTH, D, 2 * D_FF), D),
        "ffn_conv_w": w(ks[19], (DEPTH, CONV_W, 2 * D_FF), CONV_W),
        "ffn_conv_b": 0.02 * jax.random.normal(ks[20], (DEPTH, 2 * D_FF), f32),
        "ffn_w_out": w(ks[21], (DEPTH, D_FF, D), D_FF),
    }


def reference(x, mem, positions, norm_mix, norm_mem_q, norm_mem_kv, norm_ffn, norm_final,
              mla_w_down, mla_q_norm, mla_w_uq, mla_kv_norm, mla_w_ukv, mla_w_o,
              sb_w_qkv, sb_w_o, mem_w_q, mem_w_kv, mem_w_o,
              ffn_w_in, ffn_conv_w, ffn_conv_b, ffn_w_out):
    cos, sin = _rope_tables(positions, x.dtype)
    for i in range(DEPTH):
        h = _rmsnorm(x, norm_mix[i])
        j = i // N_MIXERS
        if i % N_MIXERS == 0:
            x = x + _mla(h, cos, sin, mla_w_down[j], mla_q_norm[j], mla_w_uq[j],
                         mla_kv_norm[j], mla_w_ukv[j], mla_w_o[j])
        else:
            x = x + _stick_breaking(h, sb_w_qkv[j], sb_w_o[j])
        x = x + _mem_cross_attn(_rmsnorm(x, norm_mem_q[i]), _rmsnorm(mem, norm_mem_kv[i]),
                                mem_w_q[i], mem_w_kv[i], mem_w_o[i])
        x = x + _conv_ffn(_rmsnorm(x, norm_ffn[i]), ffn_w_in[i], ffn_conv_w[i],
                          ffn_conv_b[i], ffn_w_out[i])
    return _rmsnorm(x, norm_final)
```

```python
import functools
import math

import jax
import jax.numpy as jnp
from jax import lax
from jax.experimental import pallas as pl
from jax.experimental.pallas import tpu as pltpu

F32 = jnp.float32
BF16 = jnp.bfloat16

D_MODEL = 2048
DEPTH = 4
CHUNK = 64
N_MIXERS = 2
MLA_HEADS = 16
Q_LORA = 512
KV_LORA = 512
NOPE_DIM = 128
ROPE_DIM = 64
V_DIM = 128
ROPE_THETA = 10000.0
SB_HEADS = 16
SB_HEAD_DIM = 128
N_MEM = 256
MEM_HEADS = 4
MEM_HEAD_DIM = 128
D_FF = 5632
CONV_W = 3
EPS = 1e-6

LANES = 128
SUBLANES = 8
VMEM_LIMIT = 56 * 1024 * 1024
NEG_BIG = -1e30
LOG2E = 1.4426950408889634
SB_STOP = 120.0


def _params(n_grid):
    return pltpu.CompilerParams(
        dimension_semantics=("arbitrary",) * n_grid,
        vmem_limit_bytes=VMEM_LIMIT)


def _cast_rows(dst_ref, src_ref, rows):
    n = src_ref.shape[0]
    rows = min(rows, n)

    def body(r, c):
        sl = pl.ds(pl.multiple_of(r * rows, rows), rows)
        dst_ref[sl, :] = src_ref[sl, :].astype(dst_ref.dtype)
        return c

    lax.fori_loop(0, n // rows, body, 0)


def _rms(x, g):
    ms = jnp.mean(x * x, axis=-1, keepdims=True)
    return x * lax.rsqrt(ms + EPS) * g


def _rmsnorm_kernel(x_ref, g_ref, o_ref):
    o_ref[...] = _rms(x_ref[...], g_ref[...]).astype(o_ref.dtype)


def _rmsnorm(x, g, out_dtype, tm=512):
    m, d = x.shape
    tm = min(tm, m)
    return pl.pallas_call(
        _rmsnorm_kernel,
        grid=(m // tm,),
        in_specs=[pl.BlockSpec((tm, d), lambda i: (i, 0)),
                  pl.BlockSpec((1, d), lambda i: (0, 0))],
        out_specs=pl.BlockSpec((tm, d), lambda i: (i, 0)),
        out_shape=jax.ShapeDtypeStruct((m, d), out_dtype),
        compiler_params=_params(1),
    )(x, g.reshape(1, d))


def _rope_heads(acc, cos, sin):
    outs = []
    for h in range(acc.shape[1] // LANES):
        a = acc[:, h * LANES:(h + 1) * LANES]
        outs.append(a * cos + pltpu.roll(a, LANES // 2, axis=1) * sin)
    return outs[0] if len(outs) == 1 else jnp.concatenate(outs, axis=1)


def _mm_kernel(*refs, mode, scale, scale_cols, tn):
    if mode == "rope":
        x_ref, w_ref, cos_ref, sin_ref, o_ref, wb_ref = refs
    elif mode == "residual":
        x_ref, w_ref, res_ref, o_ref, wb_ref = refs
    else:
        x_ref, w_ref, o_ref, wb_ref = refs

    @pl.when(pl.program_id(1) == 0)
    def _():
        _cast_rows(wb_ref, w_ref, 256)

    acc = jnp.dot(x_ref[...], wb_ref[...], preferred_element_type=F32)
    if mode == "rope":
        acc = _rope_heads(acc, cos_ref[...], sin_ref[...]) * scale
    elif mode == "residual":
        acc = acc + res_ref[...]
    elif mode == "scale":
        acc = acc * scale
    elif mode == "scale_cols":
        s = jnp.where(pl.program_id(0) * tn < scale_cols, scale, 1.0).astype(F32)
        acc = acc * s
    o_ref[...] = acc.astype(o_ref.dtype)


def _mm(x, w, layer, *, tm, tn, out_dtype, mode="plain", scale=1.0, scale_cols=0,
        cos=None, sin=None, res=None):
    m, k = x.shape
    n = w.shape[2]
    tm, tn = min(tm, m), min(tn, n)
    in_specs = [pl.BlockSpec((tm, k), lambda j, i: (i, 0)),
                pl.BlockSpec((None, k, tn), lambda j, i: (layer, 0, j))]
    args = [x, w]
    if mode == "rope":
        in_specs += [pl.BlockSpec((tm, LANES), lambda j, i: (i, 0))] * 2
        args += [cos, sin]
    elif mode == "residual":
        in_specs += [pl.BlockSpec((tm, tn), lambda j, i: (i, j))]
        args += [res]
    return pl.pallas_call(
        functools.partial(_mm_kernel, mode=mode, scale=scale, scale_cols=scale_cols, tn=tn),
        grid=(n // tn, m // tm),
        in_specs=in_specs,
        out_specs=pl.BlockSpec((tm, tn), lambda j, i: (i, j)),
        out_shape=jax.ShapeDtypeStruct((m, n), out_dtype),
        scratch_shapes=[pltpu.VMEM((k, tn), BF16)],
        compiler_params=_params(2),
    )(*args)


def _mla_down_kernel(h_ref, w_ref, gq_ref, gkv_ref, cos_ref, sin_ref,
                     cq_ref, ckv_ref, kr_ref, wb_ref):
    @pl.when(pl.program_id(0) == 0)
    def _():
        _cast_rows(wb_ref, w_ref, 256)

    acc = jnp.dot(h_ref[...], wb_ref[...], preferred_element_type=F32)
    cq_ref[...] = _rms(acc[:, :Q_LORA], gq_ref[...]).astype(BF16)
    ckv_ref[...] = _rms(acc[:, Q_LORA:Q_LORA + KV_LORA], gkv_ref[...]).astype(BF16)
    kr = acc[:, Q_LORA + KV_LORA:]
    kr_ref[...] = _rope_heads(kr, cos_ref[...], sin_ref[...]).astype(BF16)


def _mla_down(h, w, gq, gkv, cos, sin, tm=512):
    m, k = h.shape
    n = w.shape[1]
    tm = min(tm, m)
    row = lambda i: (i, 0)
    fixed = lambda i: (0, 0)
    return pl.pallas_call(
        _mla_down_kernel,
        grid=(m // tm,),
        in_specs=[pl.BlockSpec((tm, k), row),
                  pl.BlockSpec((k, n), fixed),
                  pl.BlockSpec((1, Q_LORA), fixed),
                  pl.BlockSpec((1, KV_LORA), fixed),
                  pl.BlockSpec((tm, LANES), row),
                  pl.BlockSpec((tm, LANES), row)],
        out_specs=[pl.BlockSpec((tm, Q_LORA), row),
                   pl.BlockSpec((tm, KV_LORA), row),
                   pl.BlockSpec((tm, LANES), row)],
        out_shape=[jax.ShapeDtypeStruct((m, Q_LORA), BF16),
                   jax.ShapeDtypeStruct((m, KV_LORA), BF16),
                   jax.ShapeDtypeStruct((m, LANES), BF16)],
        scratch_shapes=[pltpu.VMEM((k, n), BF16)],
        compiler_params=_params(1),
    )(h, w, gq.reshape(1, -1), gkv.reshape(1, -1), cos, sin)


def _mla_attn_kernel(qn_ref, qr_ref, kn_ref, kr_ref, v_ref, o_ref, *, tq):
    i = pl.program_id(1)
    q = jnp.concatenate([qn_ref[...], qr_ref[...]], axis=1)

    def step(kb, carry, masked):
        m, l, acc = carry
        rows = pl.ds(pl.multiple_of(kb * tq, tq), tq)
        k = jnp.concatenate([kn_ref[rows, :], kr_ref[rows, :]], axis=1)
        s = lax.dot_general(q, k, (((1,), (1,)), ((), ())),
                            preferred_element_type=F32)
        if masked:
            qc = lax.broadcasted_iota(jnp.int32, (tq, tq), 0) // CHUNK
            kc = lax.broadcasted_iota(jnp.int32, (tq, tq), 1) // CHUNK
            s = jnp.where(kc <= qc, s, NEG_BIG)
        m_new = jnp.maximum(m, jnp.max(s, axis=1, keepdims=True))
        alpha = jnp.exp2(m - m_new)
        p = jnp.exp2(s - m_new)
        l = alpha * l + jnp.sum(p, axis=1, keepdims=True)
        acc = alpha * acc + jnp.dot(p.astype(BF16), v_ref[rows, :],
                                    preferred_element_type=F32)
        return m_new, l, acc

    init = (jnp.full((tq, 1), NEG_BIG, F32), jnp.zeros((tq, 1), F32),
            jnp.zeros((tq, V_DIM), F32))
    carry = lax.fori_loop(0, i, lambda kb, c: step(kb, c, False), init)
    _, l, acc = step(i, carry, True)
    o_ref[...] = (acc / l).astype(o_ref.dtype)


def _mla_attn(qn, qr, kv, kr, tq=512):
    s = qn.shape[0]
    tq = min(tq, s)
    return pl.pallas_call(
        functools.partial(_mla_attn_kernel, tq=tq),
        grid=(MLA_HEADS, s // tq),
        in_specs=[pl.BlockSpec((tq, NOPE_DIM), lambda h, i: (i, h)),
                  pl.BlockSpec((tq, LANES), lambda h, i: (i, h)),
                  pl.BlockSpec((s, NOPE_DIM), lambda h, i: (0, 2 * h)),
                  pl.BlockSpec((s, LANES), lambda h, i: (0, 0)),
                  pl.BlockSpec((s, V_DIM), lambda h, i: (0, 2 * h + 1))],
        out_specs=pl.BlockSpec((tq, V_DIM), lambda h, i: (i, h)),
        out_shape=jax.ShapeDtypeStruct((s, MLA_HEADS * V_DIM), BF16),
        compiler_params=_params(2),
    )(qn, qr, kv, kr, kv)


def _sb_attn_kernel(q_ref, k_ref, v_ref, o_ref, carry_ref, acc_ref, *, tq):
    i = pl.program_id(1)
    q = q_ref[...]
    row = lax.broadcasted_iota(jnp.int32, (tq, tq), 0)
    col = lax.broadcasted_iota(jnp.int32, (tq, tq), 1)
    upper = jnp.where(row > col, 1.0, 0.0).astype(BF16)
    carry_ref[...] = jnp.zeros_like(carry_ref)
    acc_ref[...] = jnp.zeros_like(acc_ref)

    def cond(state):
        kb, live = state
        return jnp.logical_and(kb >= 0, live)

    def body(state):
        kb, _ = state
        rows = pl.ds(pl.multiple_of(kb * tq, tq), tq)
        z = lax.dot_general(q, k_ref[rows, :], (((1,), (1,)), ((), ())),
                            preferred_element_type=F32)
        sp = jnp.maximum(z, 0.0) + jnp.log1p(jnp.exp(-jnp.abs(z)))
        allowed = (kb - i) * tq + col < row
        lk = jnp.where(allowed, -sp, 0.0)
        lk_hi = lk.astype(BF16)
        lk_lo = (lk - lk_hi.astype(F32)).astype(BF16)
        later = (jnp.dot(lk_hi, upper, preferred_element_type=F32)
                 + jnp.dot(lk_lo, upper, preferred_element_type=F32))
        carry = carry_ref[...]
        a = jnp.where(allowed, jnp.exp(z - sp + later + carry), 0.0)
        acc_ref[...] += jnp.dot(a.astype(BF16), v_ref[rows, :],
                                preferred_element_type=F32)
        carry = carry + jnp.sum(lk, axis=1, keepdims=True)
        carry_ref[...] = carry
        return kb - 1, jnp.max(carry) > -SB_STOP

    lax.while_loop(cond, body, (i, True))
    o_ref[...] = acc_ref[...].astype(o_ref.dtype)


def _sb_attn(qkv, tq=256):
    s = qkv.shape[0]
    tq = min(tq, s)
    nh = SB_HEADS
    return pl.pallas_call(
        functools.partial(_sb_attn_kernel, tq=tq),
        grid=(nh, s // tq),
        in_specs=[pl.BlockSpec((tq, SB_HEAD_DIM), lambda h, i: (i, h)),
                  pl.BlockSpec((s, SB_HEAD_DIM), lambda h, i: (0, nh + h)),
                  pl.BlockSpec((s, SB_HEAD_DIM), lambda h, i: (0, 2 * nh + h))],
        out_specs=pl.BlockSpec((tq, SB_HEAD_DIM), lambda h, i: (i, h)),
        out_shape=jax.ShapeDtypeStruct((s, nh * SB_HEAD_DIM), BF16),
        scratch_shapes=[pltpu.VMEM((tq, 1), F32), pltpu.VMEM((tq, SB_HEAD_DIM), F32)],
        compiler_params=_params(2),
    )(qkv, qkv, qkv)


def _proj_res_norm_kernel(a_ref, w_ref, res_ref, g_ref, x_ref, h_ref, wb_ref):
    @pl.when(pl.program_id(0) == 0)
    def _():
        _cast_rows(wb_ref, w_ref, 256)

    x = res_ref[...] + jnp.dot(a_ref[...], wb_ref[...], preferred_element_type=F32)
    x_ref[...] = x
    h_ref[...] = _rms(x, g_ref[...]).astype(BF16)


def _proj_res_norm(a, w, layer, res, g, tm=512):
    m, k = a.shape
    n = w.shape[2]
    tm = min(tm, m)
    row = lambda i: (i, 0)
    fixed = lambda i: (0, 0)
    return pl.pallas_call(
        _proj_res_norm_kernel,
        grid=(m // tm,),
        in_specs=[pl.BlockSpec((tm, k), row),
                  pl.BlockSpec((None, k, n), lambda i: (layer, 0, 0),
                               pipeline_mode=pl.Buffered(1)),
                  pl.BlockSpec((tm, n), row),
                  pl.BlockSpec((1, n), fixed)],
        out_specs=[pl.BlockSpec((tm, n), row), pl.BlockSpec((tm, n), row)],
        out_shape=[jax.ShapeDtypeStruct((m, n), F32),
                   jax.ShapeDtypeStruct((m, n), BF16)],
        scratch_shapes=[pltpu.VMEM((k, n), BF16)],
        compiler_params=_params(1),
    )(a, w, res, g.reshape(1, n))


def _mem_kv_kernel(mem_ref, g_ref, w_ref, o_ref):
    hm = _rms(mem_ref[...], g_ref[0]).astype(BF16)
    half = w_ref.shape[2] // 2
    for c in range(2):
        w = w_ref[0, :, c * half:(c + 1) * half].astype(BF16)
        o_ref[0, :, c * half:(c + 1) * half] = jnp.dot(
            hm, w, preferred_element_type=F32).astype(BF16)


def _mem_kv(mem, g, w):
    depth, d, n = w.shape
    nm = mem.shape[0]
    return pl.pallas_call(
        _mem_kv_kernel,
        grid=(depth,),
        in_specs=[pl.BlockSpec((nm, d), lambda l: (0, 0)),
                  pl.BlockSpec((1, 1, d), lambda l: (l, 0, 0)),
                  pl.BlockSpec((1, d, n), lambda l: (l, 0, 0))],
        out_specs=pl.BlockSpec((1, nm, n), lambda l: (l, 0, 0)),
        out_shape=jax.ShapeDtypeStruct((depth, nm, n), BF16),
        compiler_params=_params(1),
    )(mem, g.reshape(depth, 1, d), w)


def _mem_attn_kernel(h_ref, x_ref, wq_ref, kv_ref, wo_ref, g_ref,
                     xo_ref, ho_ref, wqb_ref, wob_ref):
    @pl.when(pl.program_id(0) == 0)
    def _():
        _cast_rows(wqb_ref, wq_ref, 256)
        _cast_rows(wob_ref, wo_ref, 256)

    h = h_ref[...]
    hd = MEM_HEAD_DIM
    nk = MEM_HEADS * hd
    scale = LOG2E / math.sqrt(hd)
    x = x_ref[...]
    for head in range(MEM_HEADS):
        cols = slice(head * hd, (head + 1) * hd)
        q = jnp.dot(h, wqb_ref[:, cols], preferred_element_type=F32) * scale
        k = kv_ref[0, :, cols]
        v = kv_ref[0, :, nk + head * hd:nk + (head + 1) * hd]
        s = lax.dot_general(q.astype(BF16), k, (((1,), (1,)), ((), ())),
                            preferred_element_type=F32)
        p = jnp.exp2(s - jnp.max(s, axis=1, keepdims=True))
        l = jnp.sum(p, axis=1, keepdims=True)
        o = jnp.dot(p.astype(BF16), v, preferred_element_type=F32) / l
        x = x + jnp.dot(o.astype(BF16), wob_ref[cols, :], preferred_element_type=F32)
    xo_ref[...] = x
    ho_ref[...] = _rms(x, g_ref[...]).astype(BF16)


def _mem_attn(h, x, wq, kv_all, layer, wo, g, tm=512):
    m, d = h.shape
    nq = wq.shape[2]
    tm = min(tm, m)
    row = lambda i: (i, 0)
    fixed = lambda i: (0, 0)
    at_layer = lambda i: (layer, 0, 0)
    nm, nkv = kv_all.shape[1:]
    return pl.pallas_call(
        _mem_attn_kernel,
        grid=(m // tm,),
        in_specs=[pl.BlockSpec((tm, d), row),
                  pl.BlockSpec((tm, d), row),
                  pl.BlockSpec((None, d, nq), at_layer, pipeline_mode=pl.Buffered(1)),
                  pl.BlockSpec((1, nm, nkv), at_layer),
                  pl.BlockSpec((None, nq, d), at_layer, pipeline_mode=pl.Buffered(1)),
                  pl.BlockSpec((1, d), fixed)],
        out_specs=[pl.BlockSpec((tm, d), row), pl.BlockSpec((tm, d), row)],
        out_shape=[jax.ShapeDtypeStruct((m, d), F32),
                   jax.ShapeDtypeStruct((m, d), BF16)],
        scratch_shapes=[pltpu.VMEM((d, nq), BF16), pltpu.VMEM((nq, d), BF16)],
        compiler_params=_params(1),
    )(h, x, wq, kv_all, wo, g.reshape(1, d))


def _conv3(u, cw, cb):
    return (cb + cw[2:3, :] * u + cw[1:2, :] * pltpu.roll(u, 1, axis=0)
            + cw[0:1, :] * pltpu.roll(u, 2, axis=0))


def _silu_gate(gate, up):
    return gate * (1.0 / (1.0 + jnp.exp(-gate))) * up


def _ffn_in_kernel(h_ref, wg_ref, wu_ref, cwg_ref, cwu_ref, cbg_ref, cbu_ref,
                   o_ref, wgb_ref, wub_ref, halo_g_ref, halo_u_ref, *, tn, tc):
    @pl.when(pl.program_id(1) == 0)
    def _():
        _cast_rows(wgb_ref, wg_ref, 256)
        _cast_rows(wub_ref, wu_ref, 256)
        halo_g_ref[...] = jnp.zeros_like(halo_g_ref)
        halo_u_ref[...] = jnp.zeros_like(halo_u_ref)

    h = h_ref[...]
    tm = h.shape[0]
    hs = SUBLANES
    for c in range(tn // tc):
        cols = slice(c * tc, (c + 1) * tc)
        ug = jnp.dot(h, wgb_ref[:, cols], preferred_element_type=F32)
        uu = jnp.dot(h, wub_ref[:, cols], preferred_element_type=F32)
        cwg, cwu = cwg_ref[:, cols], cwu_ref[:, cols]
        cbg, cbu = cbg_ref[:, cols], cbu_ref[:, cols]
        o_ref[:, cols] = _silu_gate(_conv3(ug, cwg, cbg), _conv3(uu, cwu, cbu)).astype(o_ref.dtype)
        top_g = jnp.concatenate([halo_g_ref[:, cols], ug[:hs]], axis=0)
        top_u = jnp.concatenate([halo_u_ref[:, cols], uu[:hs]], axis=0)
        top = _silu_gate(_conv3(top_g, cwg, cbg), _conv3(top_u, cwu, cbu))
        o_ref[:hs, cols] = top[hs:].astype(o_ref.dtype)
        halo_g_ref[:, cols] = ug[tm - hs:]
        halo_u_ref[:, cols] = uu[tm - hs:]


def _ffn_in(h, w_in, conv_w, conv_b, layer, tm=1024, tn=512, tc=256):
    m, k = h.shape
    dff = w_in.shape[2] // 2
    tm = min(tm, m)
    nj = dff // tn
    conv_b = conv_b.reshape(conv_b.shape[0], 1, -1)
    gate_cols = lambda j, i: (layer, 0, j)
    up_cols = lambda j, i: (layer, 0, nj + j)
    return pl.pallas_call(
        functools.partial(_ffn_in_kernel, tn=tn, tc=tc),
        grid=(nj, m // tm),
        in_specs=[pl.BlockSpec((tm, k), lambda j, i: (i, 0)),
                  pl.BlockSpec((None, k, tn), gate_cols),
                  pl.BlockSpec((None, k, tn), up_cols),
                  pl.BlockSpec((None, CONV_W, tn), gate_cols),
                  pl.BlockSpec((None, CONV_W, tn), up_cols),
                  pl.BlockSpec((None, 1, tn), gate_cols),
                  pl.BlockSpec((None, 1, tn), up_cols)],
        out_specs=pl.BlockSpec((tm, tn), lambda j, i: (i, j)),
        out_shape=jax.ShapeDtypeStruct((m, dff), BF16),
        scratch_shapes=[pltpu.VMEM((k, tn), BF16), pltpu.VMEM((k, tn), BF16),
                        pltpu.VMEM((SUBLANES, tn), F32), pltpu.VMEM((SUBLANES, tn), F32)],
        compiler_params=_params(2),
    )(h, w_in, w_in, conv_w, conv_w, conv_b, conv_b)


def _rope_slots(t):
    half = ROPE_DIM // 2
    z = jnp.zeros(t.shape[:-1] + (LANES // 2 - half,), t.dtype)
    return jnp.concatenate([t[..., :half], z, t[..., half:], z], axis=-1)


def _rope_tables(positions):
    inv_freq = ROPE_THETA ** (-jnp.arange(0, ROPE_DIM, 2, dtype=F32) / ROPE_DIM)
    ang = positions.astype(F32)[:, None] * inv_freq
    cos, sin = jnp.cos(ang), jnp.sin(ang)
    return (_rope_slots(jnp.concatenate([cos, cos], axis=-1)),
            _rope_slots(jnp.concatenate([-sin, sin], axis=-1)))


def kernel(x, mem, positions, norm_mix, norm_mem_q, norm_mem_kv, norm_ffn, norm_final, mla_w_down, mla_q_norm, mla_w_uq, mla_kv_norm, mla_w_ukv, mla_w_o, sb_w_qkv, sb_w_o, mem_w_q, mem_w_kv, mem_w_o, ffn_w_in, ffn_conv_w, ffn_conv_b, ffn_w_out):
    assert x.shape[0] == 1, "batch is fixed to 1"
    xs = x[0]
    cos, sin = _rope_tables(positions[0])
    mem_kv = _mem_kv(mem[0], norm_mem_kv, mem_w_kv)
    mla_scale = LOG2E / math.sqrt(NOPE_DIM + ROPE_DIM)
    sb_scale = 1.0 / math.sqrt(SB_HEAD_DIM)

    h = _rmsnorm(xs, norm_mix[0], BF16)
    for i in range(DEPTH):
        j = i // N_MIXERS
        if i % N_MIXERS == 0:
            nlat = Q_LORA + KV_LORA
            w_down = jnp.concatenate(
                [mla_w_down[j][:, :nlat], _rope_slots(mla_w_down[j][:, nlat:])], axis=1)
            c_q, c_kv, k_rope = _mla_down(h, w_down, mla_q_norm[j], mla_kv_norm[j], cos, sin)
            w_uq = mla_w_uq[j].reshape(Q_LORA, MLA_HEADS, NOPE_DIM + ROPE_DIM)
            w_qn = w_uq[:, :, :NOPE_DIM].reshape(Q_LORA, MLA_HEADS * NOPE_DIM)
            w_qr = _rope_slots(w_uq[:, :, NOPE_DIM:]).reshape(Q_LORA, MLA_HEADS * LANES)
            qn = _mm(c_q, w_qn[None], 0, tm=1024, tn=1024, out_dtype=BF16, mode="scale",
                     scale=mla_scale)
            qr = _mm(c_q, w_qr[None], 0, tm=1024, tn=1024, out_dtype=BF16, mode="rope",
                     scale=mla_scale, cos=cos, sin=sin)
            kv = _mm(c_kv, mla_w_ukv, j, tm=1024, tn=1024, out_dtype=BF16)
            o = _mla_attn(qn, qr, kv, k_rope)
            w_o = mla_w_o
        else:
            qkv = _mm(h, sb_w_qkv, j, tm=1024, tn=1024, out_dtype=BF16, mode="scale_cols",
                      scale=sb_scale, scale_cols=SB_HEADS * SB_HEAD_DIM)
            o = _sb_attn(qkv)
            w_o = sb_w_o
        xs, hq = _proj_res_norm(o, w_o, j, xs, norm_mem_q[i])
        xs, hf = _mem_attn(hq, xs, mem_w_q, mem_kv, i, mem_w_o, norm_ffn[i])
        g = _ffn_in(hf, ffn_w_in, ffn_conv_w, ffn_conv_b, i)
        xs = _mm(g, ffn_w_out, i, tm=512, tn=512, out_dtype=F32, mode="residual", res=xs)
        last = i == DEPTH - 1
        h = _rmsnorm(xs, norm_final if last else norm_mix[i + 1], F32 if last else BF16)
    return h[None]
```

```python
import functools
import math

import jax
import jax.numpy as jnp
from jax import lax
from jax.experimental import pallas as pl
from jax.experimental.pallas import tpu as pltpu

F32 = jnp.float32
BF16 = jnp.bfloat16

D_MODEL = 2048
DEPTH = 4
CHUNK = 64
N_MIXERS = 2
MLA_HEADS = 16
Q_LORA = 512
KV_LORA = 512
NOPE_DIM = 128
ROPE_DIM = 64
V_DIM = 128
ROPE_THETA = 10000.0
SB_HEADS = 16
SB_HEAD_DIM = 128
N_MEM = 256
MEM_HEADS = 4
MEM_HEAD_DIM = 128
D_FF = 5632
CONV_W = 3
EPS = 1e-6

LANES = 128
SUBLANES = 8
VMEM_LIMIT = 56 * 1024 * 1024
NEG_BIG = -1e30
LOG2E = 1.4426950408889634
SB_STOP = 120.0


def _params(n_grid):
    return pltpu.CompilerParams(
        dimension_semantics=("arbitrary",) * n_grid,
        vmem_limit_bytes=VMEM_LIMIT)


def _cast_rows(dst_ref, src_ref, rows):
    n = src_ref.shape[0]
    rows = min(rows, n)

    def body(r, c):
        sl = pl.ds(pl.multiple_of(r * rows, rows), rows)
        dst_ref[sl, :] = src_ref[sl, :].astype(dst_ref.dtype)
        return c

    lax.fori_loop(0, n // rows, body, 0)


def _rms(x, g):
    ms = jnp.mean(x * x, axis=-1, keepdims=True)
    return x * lax.rsqrt(ms + EPS) * g


def _rmsnorm_kernel(x_ref, g_ref, o_ref):
    o_ref[...] = _rms(x_ref[...], g_ref[...]).astype(o_ref.dtype)


def _rmsnorm(x, g, out_dtype, tm=512):
    m, d = x.shape
    tm = min(tm, m)
    return pl.pallas_call(
        _rmsnorm_kernel,
        grid=(m // tm,),
        in_specs=[pl.BlockSpec((tm, d), lambda i: (i, 0)),
                  pl.BlockSpec((1, d), lambda i: (0, 0))],
        out_specs=pl.BlockSpec((tm, d), lambda i: (i, 0)),
        out_shape=jax.ShapeDtypeStruct((m, d), out_dtype),
        compiler_params=_params(1), name="rmsnorm",
    )(x, g.reshape(1, d))


def _rope_heads(acc, cos, sin):
    outs = []
    for h in range(acc.shape[1] // LANES):
        a = acc[:, h * LANES:(h + 1) * LANES]
        outs.append(a * cos + pltpu.roll(a, LANES // 2, axis=1) * sin)
    return outs[0] if len(outs) == 1 else jnp.concatenate(outs, axis=1)


def _mm_kernel(*refs, mode, scale, scale_cols, tn):
    if mode == "rope":
        x_ref, w_ref, cos_ref, sin_ref, o_ref, wb_ref = refs
    elif mode == "residual":
        x_ref, w_ref, res_ref, o_ref, wb_ref = refs
    else:
        x_ref, w_ref, o_ref, wb_ref = refs

    @pl.when(pl.program_id(1) == 0)
    def _():
        _cast_rows(wb_ref, w_ref, 256)

    acc = jnp.dot(x_ref[...], wb_ref[...], preferred_element_type=F32)
    if mode == "rope":
        acc = _rope_heads(acc, cos_ref[...], sin_ref[...]) * scale
    elif mode == "residual":
        acc = acc + res_ref[...]
    elif mode == "scale":
        acc = acc * scale
    elif mode == "scale_cols":
        s = jnp.where(pl.program_id(0) * tn < scale_cols, scale, 1.0).astype(F32)
        acc = acc * s
    o_ref[...] = acc.astype(o_ref.dtype)


def _mm(x, w, layer, *, tm, tn, out_dtype, mode="plain", scale=1.0, scale_cols=0,
        cos=None, sin=None, res=None):
    m, k = x.shape
    n = w.shape[2]
    tm, tn = min(tm, m), min(tn, n)
    in_specs = [pl.BlockSpec((tm, k), lambda j, i: (i, 0)),
                pl.BlockSpec((None, k, tn), lambda j, i: (layer, 0, j))]
    args = [x, w]
    if mode == "rope":
        in_specs += [pl.BlockSpec((tm, LANES), lambda j, i: (i, 0))] * 2
        args += [cos, sin]
    elif mode == "residual":
        in_specs += [pl.BlockSpec((tm, tn), lambda j, i: (i, j))]
        args += [res]
    return pl.pallas_call(
        functools.partial(_mm_kernel, mode=mode, scale=scale, scale_cols=scale_cols, tn=tn),
        grid=(n // tn, m // tm),
        in_specs=in_specs,
        out_specs=pl.BlockSpec((tm, tn), lambda j, i: (i, j)),
        out_shape=jax.ShapeDtypeStruct((m, n), out_dtype),
        scratch_shapes=[pltpu.VMEM((k, tn), BF16)],
        compiler_params=_params(2), name="mm_" + mode,
    )(*args)


def _mla_down_kernel(h_ref, w_ref, gq_ref, gkv_ref, cos_ref, sin_ref,
                     cq_ref, ckv_ref, kr_ref, wb_ref):
    @pl.when(pl.program_id(0) == 0)
    def _():
        _cast_rows(wb_ref, w_ref, 256)

    acc = jnp.dot(h_ref[...], wb_ref[...], preferred_element_type=F32)
    cq_ref[...] = _rms(acc[:, :Q_LORA], gq_ref[...]).astype(BF16)
    ckv_ref[...] = _rms(acc[:, Q_LORA:Q_LORA + KV_LORA], gkv_ref[...]).astype(BF16)
    kr = acc[:, Q_LORA + KV_LORA:]
    kr_ref[...] = _rope_heads(kr, cos_ref[...], sin_ref[...]).astype(BF16)


def _mla_down(h, w, gq, gkv, cos, sin, tm=512):
    m, k = h.shape
    n = w.shape[1]
    tm = min(tm, m)
    row = lambda i: (i, 0)
    fixed = lambda i: (0, 0)
    return pl.pallas_call(
        _mla_down_kernel,
        grid=(m // tm,),
        in_specs=[pl.BlockSpec((tm, k), row),
                  pl.BlockSpec((k, n), fixed),
                  pl.BlockSpec((1, Q_LORA), fixed),
                  pl.BlockSpec((1, KV_LORA), fixed),
                  pl.BlockSpec((tm, LANES), row),
                  pl.BlockSpec((tm, LANES), row)],
        out_specs=[pl.BlockSpec((tm, Q_LORA), row),
                   pl.BlockSpec((tm, KV_LORA), row),
                   pl.BlockSpec((tm, LANES), row)],
        out_shape=[jax.ShapeDtypeStruct((m, Q_LORA), BF16),
                   jax.ShapeDtypeStruct((m, KV_LORA), BF16),
                   jax.ShapeDtypeStruct((m, LANES), BF16)],
        scratch_shapes=[pltpu.VMEM((k, n), BF16)],
        compiler_params=_params(1), name="mla_down",
    )(h, w, gq.reshape(1, -1), gkv.reshape(1, -1), cos, sin)


def _mla_attn_kernel(qn_ref, qr_ref, kv_ref, kr_ref, o_ref,
                     s_ref, p_ref, m_ref, l_ref, acc_ref, *, tq, nh, rg):
    i = pl.program_id(1)
    hd = NOPE_DIM
    m_ref[...] = jnp.full(m_ref.shape, NEG_BIG, F32)
    l_ref[...] = jnp.zeros_like(l_ref)
    acc_ref[...] = jnp.zeros_like(acc_ref)

    def step(kb, masked):
        rows = pl.ds(pl.multiple_of(kb * tq, tq), tq)
        kr = kr_ref[rows, :]
        for h in range(nh):
            q = jnp.concatenate([qn_ref[:, h * hd:(h + 1) * hd],
                                 qr_ref[:, h * hd:(h + 1) * hd]], axis=1)
            k = jnp.concatenate([kv_ref[rows, 2 * h * hd:(2 * h + 1) * hd], kr], axis=1)
            s = lax.dot_general(q, k, (((1,), (1,)), ((), ())),
                                preferred_element_type=F32)
            if masked:
                qc = lax.broadcasted_iota(jnp.int32, (tq, tq), 0) // CHUNK
                kc = lax.broadcasted_iota(jnp.int32, (tq, tq), 1) // CHUNK
                s = jnp.where(kc <= qc, s, NEG_BIG)
            s_ref[h] = s
        nt = tq // LANES
        alphas = []
        for h in range(nh):
            m_tile = s_ref[h, :, 0:LANES]
            for c in range(1, nt):
                m_tile = jnp.maximum(m_tile, s_ref[h, :, c * LANES:(c + 1) * LANES])
            m_prev = m_ref[h]
            m_rep = jnp.maximum(m_prev, jnp.max(m_tile, axis=1, keepdims=True))
            m_ref[h] = m_rep
            alpha = jnp.exp2(m_prev - m_rep)
            alphas.append(alpha)
            for r in range(tq // rg):
                rs = slice(r * rg, (r + 1) * rg)
                part = None
                for c in range(nt):
                    cs = slice(c * LANES, (c + 1) * LANES)
                    p = jnp.exp2(s_ref[h, rs, cs] - m_rep[rs])
                    p_ref[h, rs, cs] = p.astype(BF16)
                    part = p if part is None else part + p
                l_ref[h, rs, :] = alpha[rs] * l_ref[h, rs, :] + part
        for h in range(nh):
            v = kv_ref[rows, (2 * h + 1) * hd:(2 * h + 2) * hd]
            acc_ref[h] = alphas[h] * acc_ref[h] + jnp.dot(p_ref[h], v, preferred_element_type=F32)

    def body(kb, c):
        step(kb, False)
        return c

    lax.fori_loop(0, i, body, 0)
    step(i, True)
    for h in range(nh):
        l = jnp.sum(l_ref[h], axis=1, keepdims=True)
        o_ref[:, h * hd:(h + 1) * hd] = (acc_ref[h] / l).astype(o_ref.dtype)


def _mla_attn(qn, qr, kv, kr, tq=512, nh=2, rg=32):
    s = qn.shape[0]
    tq = min(tq, s)
    assert CHUNK % rg == 0 and tq % CHUNK == 0
    return pl.pallas_call(
        functools.partial(_mla_attn_kernel, tq=tq, nh=nh, rg=rg),
        scratch_shapes=[pltpu.VMEM((nh, tq, tq), F32), pltpu.VMEM((nh, tq, tq), BF16),
                        pltpu.VMEM((nh, tq, LANES), F32), pltpu.VMEM((nh, tq, LANES), F32),
                        pltpu.VMEM((nh, tq, V_DIM), F32)],
        grid=(MLA_HEADS // nh, s // tq),
        in_specs=[pl.BlockSpec((tq, nh * NOPE_DIM), lambda g, i: (i, g)),
                  pl.BlockSpec((tq, nh * LANES), lambda g, i: (i, g)),
                  pl.BlockSpec((s, nh * (NOPE_DIM + V_DIM)), lambda g, i: (0, g)),
                  pl.BlockSpec((s, LANES), lambda g, i: (0, 0))],
        out_specs=pl.BlockSpec((tq, nh * V_DIM), lambda g, i: (i, g)),
        out_shape=jax.ShapeDtypeStruct((s, MLA_HEADS * V_DIM), BF16),
        compiler_params=_params(2), name="mla_attn",
    )(qn, qr, kv, kr)


def _sb_attn_kernel(q_ref, k_ref, v_ref, o_ref, carry_ref, acc_ref, *, tq, nh):
    i = pl.program_id(1)
    hd = SB_HEAD_DIM
    row = lax.broadcasted_iota(jnp.int32, (tq, tq), 0)
    col = lax.broadcasted_iota(jnp.int32, (tq, tq), 1)
    upper = jnp.where(row > col, 1.0, 0.0).astype(BF16)
    carry_ref[...] = jnp.zeros_like(carry_ref)
    acc_ref[...] = jnp.zeros_like(acc_ref)

    def step(kb, diagonal):
        rows = pl.ds(pl.multiple_of(kb * tq, tq), tq)
        heads = range(nh)
        hcols = [slice(h * hd, (h + 1) * hd) for h in heads]
        z = [lax.dot_general(q_ref[:, hcols[h]], k_ref[rows, hcols[h]], (((1,), (1,)), ((), ())),
                             preferred_element_type=F32) for h in heads]
        sp = [jnp.maximum(z[h], 0.0) + jnp.log(1.0 + jnp.exp(-jnp.abs(z[h]))) for h in heads]
        lk = [jnp.where(col < row, -sp[h], 0.0) if diagonal else -sp[h] for h in heads]
        later = []
        for h in heads:
            lk_hi = lk[h].astype(BF16)
            lk_lo = (lk[h] - lk_hi.astype(F32)).astype(BF16)
            later.append(jnp.dot(lk_hi, upper, preferred_element_type=F32)
                         + jnp.dot(lk_lo, upper, preferred_element_type=F32))
        top = None
        for h in heads:
            carry = carry_ref[h]
            a = jnp.exp(z[h] - sp[h] + later[h] + jnp.tile(carry, (1, tq // LANES)))
            if diagonal:
                a = jnp.where(col < row, a, 0.0)
            acc_ref[:, hcols[h]] += jnp.dot(a.astype(BF16), v_ref[rows, hcols[h]],
                                            preferred_element_type=F32)
            carry = carry + jnp.sum(lk[h], axis=1, keepdims=True)
            carry_ref[h] = carry
            top = carry if top is None else jnp.maximum(top, carry)
        return jnp.max(top) > -SB_STOP

    def cond(state):
        kb, live = state
        return jnp.logical_and(kb >= 0, live)

    def body(state):
        kb, _ = state
        return kb - 1, step(kb, False)

    lax.while_loop(cond, body, (i - 1, step(i, True)))
    o_ref[...] = acc_ref[...].astype(o_ref.dtype)


def _sb_attn(qkv, tq=256, nh=4):
    s = qkv.shape[0]
    tq = min(tq, s)
    ng = SB_HEADS // nh
    w = nh * SB_HEAD_DIM
    return pl.pallas_call(
        functools.partial(_sb_attn_kernel, tq=tq, nh=nh),
        grid=(ng, s // tq),
        in_specs=[pl.BlockSpec((tq, w), lambda g, i: (i, g)),
                  pl.BlockSpec((s, w), lambda g, i: (0, ng + g)),
                  pl.BlockSpec((s, w), lambda g, i: (0, 2 * ng + g))],
        out_specs=pl.BlockSpec((tq, w), lambda g, i: (i, g)),
        out_shape=jax.ShapeDtypeStruct((s, SB_HEADS * SB_HEAD_DIM), BF16),
        scratch_shapes=[pltpu.VMEM((nh, tq, LANES), F32), pltpu.VMEM((tq, w), F32)],
        compiler_params=_params(2), name="sb_attn",
    )(qkv, qkv, qkv)


def _proj_res_norm_kernel(a_ref, w_ref, res_ref, g_ref, x_ref, h_ref, wb_ref):
    @pl.when(pl.program_id(0) == 0)
    def _():
        _cast_rows(wb_ref, w_ref, 256)

    x = res_ref[...] + jnp.dot(a_ref[...], wb_ref[...], preferred_element_type=F32)
    x_ref[...] = x
    h_ref[...] = _rms(x, g_ref[...]).astype(BF16)


def _proj_res_norm(a, w, layer, res, g, tm=512):
    m, k = a.shape
    n = w.shape[2]
    tm = min(tm, m)
    row = lambda i: (i, 0)
    fixed = lambda i: (0, 0)
    return pl.pallas_call(
        _proj_res_norm_kernel,
        grid=(m // tm,),
        in_specs=[pl.BlockSpec((tm, k), row),
                  pl.BlockSpec((None, k, n), lambda i: (layer, 0, 0),
                               pipeline_mode=pl.Buffered(1)),
                  pl.BlockSpec((tm, n), row),
                  pl.BlockSpec((1, n), fixed)],
        out_specs=[pl.BlockSpec((tm, n), row), pl.BlockSpec((tm, n), row)],
        out_shape=[jax.ShapeDtypeStruct((m, n), F32),
                   jax.ShapeDtypeStruct((m, n), BF16)],
        scratch_shapes=[pltpu.VMEM((k, n), BF16)],
        compiler_params=_params(1), name="proj_res_norm",
    )(a, w, res, g.reshape(1, n))


def _mem_kv_kernel(mem_ref, g_ref, w_ref, o_ref):
    hm = _rms(mem_ref[...], g_ref[0]).astype(BF16)
    half = w_ref.shape[2] // 2
    for c in range(2):
        w = w_ref[0, :, c * half:(c + 1) * half].astype(BF16)
        o_ref[0, :, c * half:(c + 1) * half] = jnp.dot(
            hm, w, preferred_element_type=F32).astype(BF16)


def _mem_kv(mem, g, w):
    depth, d, n = w.shape
    nm = mem.shape[0]
    return pl.pallas_call(
        _mem_kv_kernel,
        grid=(depth,),
        in_specs=[pl.BlockSpec((nm, d), lambda l: (0, 0)),
                  pl.BlockSpec((1, 1, d), lambda l: (l, 0, 0)),
                  pl.BlockSpec((1, d, n), lambda l: (l, 0, 0))],
        out_specs=pl.BlockSpec((1, nm, n), lambda l: (l, 0, 0)),
        out_shape=jax.ShapeDtypeStruct((depth, nm, n), BF16),
        compiler_params=_params(1), name="mem_kv",
    )(mem, g.reshape(depth, 1, d), w)


def _mem_attn_kernel(h_ref, x_ref, wq_ref, kv_ref, wo_ref, g_ref,
                     xo_ref, ho_ref, wqb_ref, wob_ref):
    @pl.when(pl.program_id(0) == 0)
    def _():
        _cast_rows(wqb_ref, wq_ref, 256)
        _cast_rows(wob_ref, wo_ref, 256)

    h = h_ref[...]
    hd = MEM_HEAD_DIM
    nk = MEM_HEADS * hd
    scale = LOG2E / math.sqrt(hd)
    x = x_ref[...]
    for head in range(MEM_HEADS):
        cols = slice(head * hd, (head + 1) * hd)
        q = jnp.dot(h, wqb_ref[:, cols], preferred_element_type=F32) * scale
        k = kv_ref[0, :, cols]
        v = kv_ref[0, :, nk + head * hd:nk + (head + 1) * hd]
        s = lax.dot_general(q.astype(BF16), k, (((1,), (1,)), ((), ())),
                            preferred_element_type=F32)
        p = jnp.exp2(s - jnp.max(s, axis=1, keepdims=True))
        l = jnp.sum(p, axis=1, keepdims=True)
        o = jnp.dot(p.astype(BF16), v, preferred_element_type=F32) / l
        x = x + jnp.dot(o.astype(BF16), wob_ref[cols, :], preferred_element_type=F32)
    xo_ref[...] = x
    ho_ref[...] = _rms(x, g_ref[...]).astype(BF16)


def _mem_attn(h, x, wq, kv_all, layer, wo, g, tm=512):
    m, d = h.shape
    nq = wq.shape[2]
    tm = min(tm, m)
    row = lambda i: (i, 0)
    fixed = lambda i: (0, 0)
    at_layer = lambda i: (layer, 0, 0)
    nm, nkv = kv_all.shape[1:]
    return pl.pallas_call(
        _mem_attn_kernel,
        grid=(m // tm,),
        in_specs=[pl.BlockSpec((tm, d), row),
                  pl.BlockSpec((tm, d), row),
                  pl.BlockSpec((None, d, nq), at_layer, pipeline_mode=pl.Buffered(1)),
                  pl.BlockSpec((1, nm, nkv), at_layer),
                  pl.BlockSpec((None, nq, d), at_layer, pipeline_mode=pl.Buffered(1)),
                  pl.BlockSpec((1, d), fixed)],
        out_specs=[pl.BlockSpec((tm, d), row), pl.BlockSpec((tm, d), row)],
        out_shape=[jax.ShapeDtypeStruct((m, d), F32),
                   jax.ShapeDtypeStruct((m, d), BF16)],
        scratch_shapes=[pltpu.VMEM((d, nq), BF16), pltpu.VMEM((nq, d), BF16)],
        compiler_params=_params(1), name="mem_attn",
    )(h, x, wq, kv_all, wo, g.reshape(1, d))


def _conv3(u, cw, cb):
    return (cb + cw[2:3, :] * u + cw[1:2, :] * pltpu.roll(u, 1, axis=0)
            + cw[0:1, :] * pltpu.roll(u, 2, axis=0))


def _silu_gate(gate, up):
    return gate * (1.0 / (1.0 + jnp.exp(-gate))) * up


def _ffn_in_kernel(h_ref, wg_ref, wu_ref, cwg_ref, cwu_ref, cbg_ref, cbu_ref,
                   o_ref, wgb_ref, wub_ref, halo_g_ref, halo_u_ref, *, tn, tc):
    @pl.when(pl.program_id(1) == 0)
    def _():
        _cast_rows(wgb_ref, wg_ref, 256)
        _cast_rows(wub_ref, wu_ref, 256)
        halo_g_ref[...] = jnp.zeros_like(halo_g_ref)
        halo_u_ref[...] = jnp.zeros_like(halo_u_ref)

    h = h_ref[...]
    tm = h.shape[0]
    hs = SUBLANES
    for c in range(tn // tc):
        cols = slice(c * tc, (c + 1) * tc)
        ug = jnp.dot(h, wgb_ref[:, cols], preferred_element_type=F32)
        uu = jnp.dot(h, wub_ref[:, cols], preferred_element_type=F32)
        cwg, cwu = cwg_ref[:, cols], cwu_ref[:, cols]
        cbg, cbu = cbg_ref[:, cols], cbu_ref[:, cols]
        o_ref[:, cols] = _silu_gate(_conv3(ug, cwg, cbg), _conv3(uu, cwu, cbu)).astype(o_ref.dtype)
        top_g = jnp.concatenate([halo_g_ref[:, cols], ug[:hs]], axis=0)
        top_u = jnp.concatenate([halo_u_ref[:, cols], uu[:hs]], axis=0)
        top = _silu_gate(_conv3(top_g, cwg, cbg), _conv3(top_u, cwu, cbu))
        o_ref[:hs, cols] = top[hs:].astype(o_ref.dtype)
        halo_g_ref[:, cols] = ug[tm - hs:]
        halo_u_ref[:, cols] = uu[tm - hs:]


def _ffn_in(h, w_in, conv_w, conv_b, layer, tm=1024, tn=512, tc=256):
    m, k = h.shape
    dff = w_in.shape[2] // 2
    tm = min(tm, m)
    nj = dff // tn
    conv_b = conv_b.reshape(conv_b.shape[0], 1, -1)
    gate_cols = lambda j, i: (layer, 0, j)
    up_cols = lambda j, i: (layer, 0, nj + j)
    return pl.pallas_call(
        functools.partial(_ffn_in_kernel, tn=tn, tc=tc),
        grid=(nj, m // tm),
        in_specs=[pl.BlockSpec((tm, k), lambda j, i: (i, 0)),
                  pl.BlockSpec((None, k, tn), gate_cols),
                  pl.BlockSpec((None, k, tn), up_cols),
                  pl.BlockSpec((None, CONV_W, tn), gate_cols),
                  pl.BlockSpec((None, CONV_W, tn), up_cols),
                  pl.BlockSpec((None, 1, tn), gate_cols),
                  pl.BlockSpec((None, 1, tn), up_cols)],
        out_specs=pl.BlockSpec((tm, tn), lambda j, i: (i, j)),
        out_shape=jax.ShapeDtypeStruct((m, dff), BF16),
        scratch_shapes=[pltpu.VMEM((k, tn), BF16), pltpu.VMEM((k, tn), BF16),
                        pltpu.VMEM((SUBLANES, tn), F32), pltpu.VMEM((SUBLANES, tn), F32)],
        compiler_params=_params(2), name="ffn_in",
    )(h, w_in, w_in, conv_w, conv_w, conv_b, conv_b)


def _rope_slots(t):
    half = ROPE_DIM // 2
    z = jnp.zeros(t.shape[:-1] + (LANES // 2 - half,), t.dtype)
    return jnp.concatenate([t[..., :half], z, t[..., half:], z], axis=-1)


def _rope_tables(positions):
    inv_freq = ROPE_THETA ** (-jnp.arange(0, ROPE_DIM, 2, dtype=F32) / ROPE_DIM)
    ang = positions.astype(F32)[:, None] * inv_freq
    cos, sin = jnp.cos(ang), jnp.sin(ang)
    return (_rope_slots(jnp.concatenate([cos, cos], axis=-1)),
            _rope_slots(jnp.concatenate([-sin, sin], axis=-1)))


def kernel(x, mem, positions, norm_mix, norm_mem_q, norm_mem_kv, norm_ffn, norm_final, mla_w_down, mla_q_norm, mla_w_uq, mla_kv_norm, mla_w_ukv, mla_w_o, sb_w_qkv, sb_w_o, mem_w_q, mem_w_kv, mem_w_o, ffn_w_in, ffn_conv_w, ffn_conv_b, ffn_w_out):
    assert x.shape[0] == 1, "batch is fixed to 1"
    xs = x[0]
    cos, sin = _rope_tables(positions[0])
    mem_kv = _mem_kv(mem[0], norm_mem_kv, mem_w_kv)
    mla_scale = LOG2E / math.sqrt(NOPE_DIM + ROPE_DIM)
    sb_scale = 1.0 / math.sqrt(SB_HEAD_DIM)

    h = _rmsnorm(xs, norm_mix[0], BF16)
    for i in range(DEPTH):
        j = i // N_MIXERS
        if i % N_MIXERS == 0:
            nlat = Q_LORA + KV_LORA
            w_down = jnp.concatenate(
                [mla_w_down[j][:, :nlat], _rope_slots(mla_w_down[j][:, nlat:])], axis=1)
            c_q, c_kv, k_rope = _mla_down(h, w_down, mla_q_norm[j], mla_kv_norm[j], cos, sin)
            w_uq = mla_w_uq[j].reshape(Q_LORA, MLA_HEADS, NOPE_DIM + ROPE_DIM)
            w_qn = w_uq[:, :, :NOPE_DIM].reshape(Q_LORA, MLA_HEADS * NOPE_DIM)
            w_qr = _rope_slots(w_uq[:, :, NOPE_DIM:]).reshape(Q_LORA, MLA_HEADS * LANES)
            qn = _mm(c_q, w_qn[None], 0, tm=1024, tn=1024, out_dtype=BF16, mode="scale",
                     scale=mla_scale)
            qr = _mm(c_q, w_qr[None], 0, tm=1024, tn=1024, out_dtype=BF16, mode="rope",
                     scale=mla_scale, cos=cos, sin=sin)
            kv = _mm(c_kv, mla_w_ukv, j, tm=1024, tn=1024, out_dtype=BF16)
            o = _mla_attn(qn, qr, kv, k_rope)
            w_o = mla_w_o
        else:
            qkv = _mm(h, sb_w_qkv, j, tm=1024, tn=1024, out_dtype=BF16, mode="scale_cols",
                      scale=sb_scale, scale_cols=SB_HEADS * SB_HEAD_DIM)
            o = _sb_attn(qkv)
            w_o = sb_w_o
        xs, hq = _proj_res_norm(o, w_o, j, xs, norm_mem_q[i])
        xs, hf = _mem_attn(hq, xs, mem_w_q, mem_kv, i, mem_w_o, norm_ffn[i])
        g = _ffn_in(hf, ffn_w_in, ffn_conv_w, ffn_conv_b, i)
        xs = _mm(g, ffn_w_out, i, tm=512, tn=512, out_dtype=F32, mode="residual", res=xs)
        last = i == DEPTH - 1
        h = _rmsnorm(xs, norm_final if last else norm_mix[i + 1], F32 if last else BF16)
    return h[None]
```

```python
import functools
import math

import jax
import jax.numpy as jnp
from jax import lax
from jax.experimental import pallas as pl
from jax.experimental.pallas import tpu as pltpu

F32 = jnp.float32
BF16 = jnp.bfloat16

D_MODEL = 2048
DEPTH = 4
CHUNK = 64
N_MIXERS = 2
MLA_HEADS = 16
Q_LORA = 512
KV_LORA = 512
NOPE_DIM = 128
ROPE_DIM = 64
V_DIM = 128
ROPE_THETA = 10000.0
SB_HEADS = 16
SB_HEAD_DIM = 128
N_MEM = 256
MEM_HEADS = 4
MEM_HEAD_DIM = 128
D_FF = 5632
CONV_W = 3
EPS = 1e-6

LANES = 128
SUBLANES = 8
VMEM_LIMIT = 56 * 1024 * 1024
NEG_BIG = -1e30
LOG2E = 1.4426950408889634
SB_STOP = 120.0


def _params(n_grid, flags=None):
    return pltpu.CompilerParams(
        dimension_semantics=("arbitrary",) * n_grid,
        vmem_limit_bytes=VMEM_LIMIT, flags=flags)


def _cast_rows(dst_ref, src_ref, rows):
    n = src_ref.shape[0]
    rows = min(rows, n)

    def body(r, c):
        sl = pl.ds(pl.multiple_of(r * rows, rows), rows)
        dst_ref[sl, :] = src_ref[sl, :].astype(dst_ref.dtype)
        return c

    lax.fori_loop(0, n // rows, body, 0)


def _rms(x, g):
    ms = jnp.mean(x * x, axis=-1, keepdims=True)
    return x * lax.rsqrt(ms + EPS) * g


def _rmsnorm_kernel(x_ref, g_ref, o_ref):
    o_ref[...] = _rms(x_ref[...], g_ref[...]).astype(o_ref.dtype)


def _rmsnorm(x, g, out_dtype, tm=512):
    m, d = x.shape
    tm = min(tm, m)
    return pl.pallas_call(
        _rmsnorm_kernel,
        grid=(m // tm,),
        in_specs=[pl.BlockSpec((tm, d), lambda i: (i, 0)),
                  pl.BlockSpec((1, d), lambda i: (0, 0))],
        out_specs=pl.BlockSpec((tm, d), lambda i: (i, 0)),
        out_shape=jax.ShapeDtypeStruct((m, d), out_dtype),
        compiler_params=_params(1), name="rmsnorm",
    )(x, g.reshape(1, d))


def _rope_heads(acc, cos, sin):
    outs = []
    for h in range(acc.shape[1] // LANES):
        a = acc[:, h * LANES:(h + 1) * LANES]
        outs.append(a * cos + pltpu.roll(a, LANES // 2, axis=1) * sin)
    return outs[0] if len(outs) == 1 else jnp.concatenate(outs, axis=1)


def _mm_kernel(*refs, mode, scale, scale_cols, tn):
    if mode == "rope":
        x_ref, w_ref, cos_ref, sin_ref, o_ref, wb_ref = refs
    elif mode == "residual":
        x_ref, w_ref, res_ref, o_ref, wb_ref = refs
    else:
        x_ref, w_ref, o_ref, wb_ref = refs

    @pl.when(pl.program_id(1) == 0)
    def _():
        _cast_rows(wb_ref, w_ref, 256)

    acc = jnp.dot(x_ref[...], wb_ref[...], preferred_element_type=F32)
    if mode == "rope":
        acc = _rope_heads(acc, cos_ref[...], sin_ref[...]) * scale
    elif mode == "residual":
        acc = acc + res_ref[...]
    elif mode == "scale":
        acc = acc * scale
    elif mode == "scale_cols":
        s = jnp.where(pl.program_id(0) * tn < scale_cols, scale, 1.0).astype(F32)
        acc = acc * s
    o_ref[...] = acc.astype(o_ref.dtype)


def _mm(x, w, layer, *, tm, tn, out_dtype, mode="plain", scale=1.0, scale_cols=0,
        cos=None, sin=None, res=None):
    m, k = x.shape
    n = w.shape[2]
    tm, tn = min(tm, m), min(tn, n)
    in_specs = [pl.BlockSpec((tm, k), lambda j, i: (i, 0)),
                pl.BlockSpec((None, k, tn), lambda j, i: (layer, 0, j))]
    args = [x, w]
    if mode == "rope":
        in_specs += [pl.BlockSpec((tm, LANES), lambda j, i: (i, 0))] * 2
        args += [cos, sin]
    elif mode == "residual":
        in_specs += [pl.BlockSpec((tm, tn), lambda j, i: (i, j))]
        args += [res]
    return pl.pallas_call(
        functools.partial(_mm_kernel, mode=mode, scale=scale, scale_cols=scale_cols, tn=tn),
        grid=(n // tn, m // tm),
        in_specs=in_specs,
        out_specs=pl.BlockSpec((tm, tn), lambda j, i: (i, j)),
        out_shape=jax.ShapeDtypeStruct((m, n), out_dtype),
        scratch_shapes=[pltpu.VMEM((k, tn), BF16)],
        compiler_params=_params(2), name="mm_" + mode,
    )(*args)


def _mla_down_kernel(h_ref, w_ref, gq_ref, gkv_ref, cos_ref, sin_ref,
                     cq_ref, ckv_ref, kr_ref, wb_ref):
    @pl.when(pl.program_id(0) == 0)
    def _():
        _cast_rows(wb_ref, w_ref, 256)

    acc = jnp.dot(h_ref[...], wb_ref[...], preferred_element_type=F32)
    cq_ref[...] = _rms(acc[:, :Q_LORA], gq_ref[...]).astype(BF16)
    ckv_ref[...] = _rms(acc[:, Q_LORA:Q_LORA + KV_LORA], gkv_ref[...]).astype(BF16)
    kr = acc[:, Q_LORA + KV_LORA:]
    kr_ref[...] = _rope_heads(kr, cos_ref[...], sin_ref[...]).astype(BF16)


def _mla_down(h, w, gq, gkv, cos, sin, tm=512):
    m, k = h.shape
    n = w.shape[1]
    tm = min(tm, m)
    row = lambda i: (i, 0)
    fixed = lambda i: (0, 0)
    return pl.pallas_call(
        _mla_down_kernel,
        grid=(m // tm,),
        in_specs=[pl.BlockSpec((tm, k), row),
                  pl.BlockSpec((k, n), fixed),
                  pl.BlockSpec((1, Q_LORA), fixed),
                  pl.BlockSpec((1, KV_LORA), fixed),
                  pl.BlockSpec((tm, LANES), row),
                  pl.BlockSpec((tm, LANES), row)],
        out_specs=[pl.BlockSpec((tm, Q_LORA), row),
                   pl.BlockSpec((tm, KV_LORA), row),
                   pl.BlockSpec((tm, LANES), row)],
        out_shape=[jax.ShapeDtypeStruct((m, Q_LORA), BF16),
                   jax.ShapeDtypeStruct((m, KV_LORA), BF16),
                   jax.ShapeDtypeStruct((m, LANES), BF16)],
        scratch_shapes=[pltpu.VMEM((k, n), BF16)],
        compiler_params=_params(1), name="mla_down",
    )(h, w, gq.reshape(1, -1), gkv.reshape(1, -1), cos, sin)


def _mla_attn_kernel(qn_ref, qr_ref, kv_ref, kr_ref, o_ref,
                     s_ref, p_ref, m_ref, l_ref, acc_ref, *, tq, nh, rg):
    i = pl.program_id(1)
    hd = NOPE_DIM
    m_ref[...] = jnp.full(m_ref.shape, NEG_BIG, F32)
    l_ref[...] = jnp.zeros_like(l_ref)
    acc_ref[...] = jnp.zeros_like(acc_ref)

    def step(kb, masked):
        rows = pl.ds(pl.multiple_of(kb * tq, tq), tq)
        kr = kr_ref[rows, :]
        for h in range(nh):
            q = jnp.concatenate([qn_ref[:, h * hd:(h + 1) * hd],
                                 qr_ref[:, h * hd:(h + 1) * hd]], axis=1)
            k = jnp.concatenate([kv_ref[rows, 2 * h * hd:(2 * h + 1) * hd], kr], axis=1)
            s = lax.dot_general(q, k, (((1,), (1,)), ((), ())),
                                preferred_element_type=F32)
            if masked:
                qc = lax.broadcasted_iota(jnp.int32, (tq, tq), 0) // CHUNK
                kc = lax.broadcasted_iota(jnp.int32, (tq, tq), 1) // CHUNK
                s = jnp.where(kc <= qc, s, NEG_BIG)
            s_ref[h] = s
        nt = tq // LANES
        alphas = []
        for h in range(nh):
            m_tile = s_ref[h, :, 0:LANES]
            for c in range(1, nt):
                m_tile = jnp.maximum(m_tile, s_ref[h, :, c * LANES:(c + 1) * LANES])
            m_prev = m_ref[h]
            m_rep = jnp.maximum(m_prev, jnp.max(m_tile, axis=1, keepdims=True))
            m_ref[h] = m_rep
            alpha = jnp.exp2(m_prev - m_rep)
            alphas.append(alpha)
            for r in range(tq // rg):
                rs = slice(r * rg, (r + 1) * rg)
                part = None
                for c in range(nt):
                    cs = slice(c * LANES, (c + 1) * LANES)
                    p = jnp.exp2(s_ref[h, rs, cs] - m_rep[rs])
                    p_ref[h, rs, cs] = p.astype(BF16)
                    part = p if part is None else part + p
                l_ref[h, rs, :] = alpha[rs] * l_ref[h, rs, :] + part
        for h in range(nh):
            v = kv_ref[rows, (2 * h + 1) * hd:(2 * h + 2) * hd]
            acc_ref[h] = alphas[h] * acc_ref[h] + jnp.dot(p_ref[h], v, preferred_element_type=F32)

    def body(kb, c):
        step(kb, False)
        return c

    lax.fori_loop(0, i, body, 0)
    step(i, True)
    for h in range(nh):
        l = jnp.sum(l_ref[h], axis=1, keepdims=True)
        o_ref[:, h * hd:(h + 1) * hd] = (acc_ref[h] / l).astype(o_ref.dtype)


def _mla_attn(qn, qr, kv, kr, tq=512, nh=4, rg=32):
    s = qn.shape[0]
    tq = min(tq, s)
    assert CHUNK % rg == 0 and tq % CHUNK == 0
    return pl.pallas_call(
        functools.partial(_mla_attn_kernel, tq=tq, nh=nh, rg=rg),
        scratch_shapes=[pltpu.VMEM((nh, tq, tq), F32), pltpu.VMEM((nh, tq, tq), BF16),
                        pltpu.VMEM((nh, tq, LANES), F32), pltpu.VMEM((nh, tq, LANES), F32),
                        pltpu.VMEM((nh, tq, V_DIM), F32)],
        grid=(MLA_HEADS // nh, s // tq),
        in_specs=[pl.BlockSpec((tq, nh * NOPE_DIM), lambda g, i: (i, g)),
                  pl.BlockSpec((tq, nh * LANES), lambda g, i: (i, g)),
                  pl.BlockSpec((s, nh * (NOPE_DIM + V_DIM)), lambda g, i: (0, g)),
                  pl.BlockSpec((s, LANES), lambda g, i: (0, 0))],
        out_specs=pl.BlockSpec((tq, nh * V_DIM), lambda g, i: (i, g)),
        out_shape=jax.ShapeDtypeStruct((s, MLA_HEADS * V_DIM), BF16),
        compiler_params=_params(2), name="mla_attn",
    )(qn, qr, kv, kr)


def _sb_attn_kernel(q_ref, k_ref, v_ref, o_ref, carry_ref, acc_ref, *, tq, nh):
    i = pl.program_id(1)
    hd = SB_HEAD_DIM
    row = lax.broadcasted_iota(jnp.int32, (tq, tq), 0)
    col = lax.broadcasted_iota(jnp.int32, (tq, tq), 1)
    upper = jnp.where(row > col, 1.0, 0.0).astype(BF16)
    carry_ref[...] = jnp.zeros_like(carry_ref)
    acc_ref[...] = jnp.zeros_like(acc_ref)

    def step(kb, diagonal):
        rows = pl.ds(pl.multiple_of(kb * tq, tq), tq)
        heads = range(nh)
        hcols = [slice(h * hd, (h + 1) * hd) for h in heads]
        z = [lax.dot_general(q_ref[:, hcols[h]], k_ref[rows, hcols[h]], (((1,), (1,)), ((), ())),
                             preferred_element_type=F32) for h in heads]
        sp = [jnp.maximum(z[h], 0.0) + jnp.log(1.0 + jnp.exp(-jnp.abs(z[h]))) for h in heads]
        lk = [jnp.where(col < row, -sp[h], 0.0) if diagonal else -sp[h] for h in heads]
        later = []
        for h in heads:
            lk_hi = lk[h].astype(BF16)
            lk_lo = (lk[h] - lk_hi.astype(F32)).astype(BF16)
            later.append(jnp.dot(lk_hi, upper, preferred_element_type=F32)
                         + jnp.dot(lk_lo, upper, preferred_element_type=F32))
        top = None
        for h in heads:
            carry = carry_ref[h]
            a = jnp.exp(z[h] - sp[h] + later[h] + jnp.tile(carry, (1, tq // LANES)))
            if diagonal:
                a = jnp.where(col < row, a, 0.0)
            acc_ref[:, hcols[h]] += jnp.dot(a.astype(BF16), v_ref[rows, hcols[h]],
                                            preferred_element_type=F32)
            carry = carry + jnp.sum(lk[h], axis=1, keepdims=True)
            carry_ref[h] = carry
            top = carry if top is None else jnp.maximum(top, carry)
        return jnp.max(top) > -SB_STOP

    def cond(state):
        kb, live = state
        return jnp.logical_and(kb >= 0, live)

    def body(state):
        kb, _ = state
        return kb - 1, step(kb, False)

    lax.while_loop(cond, body, (i - 1, step(i, True)))
    o_ref[...] = acc_ref[...].astype(o_ref.dtype)


def _sb_attn(qkv, tq=256, nh=4):
    s = qkv.shape[0]
    tq = min(tq, s)
    ng = SB_HEADS // nh
    w = nh * SB_HEAD_DIM
    return pl.pallas_call(
        functools.partial(_sb_attn_kernel, tq=tq, nh=nh),
        grid=(ng, s // tq),
        in_specs=[pl.BlockSpec((tq, w), lambda g, i: (i, g)),
                  pl.BlockSpec((s, w), lambda g, i: (0, ng + g)),
                  pl.BlockSpec((s, w), lambda g, i: (0, 2 * ng + g))],
        out_specs=pl.BlockSpec((tq, w), lambda g, i: (i, g)),
        out_shape=jax.ShapeDtypeStruct((s, SB_HEADS * SB_HEAD_DIM), BF16),
        scratch_shapes=[pltpu.VMEM((nh, tq, LANES), F32), pltpu.VMEM((tq, w), F32)],
        compiler_params=_params(2), name="sb_attn",
    )(qkv, qkv, qkv)


def _proj_res_norm_kernel(a_ref, w_ref, res_ref, g_ref, x_ref, h_ref, wb_ref):
    @pl.when(pl.program_id(0) == 0)
    def _():
        _cast_rows(wb_ref, w_ref, 256)

    x = res_ref[...] + jnp.dot(a_ref[...], wb_ref[...], preferred_element_type=F32)
    x_ref[...] = x
    h_ref[...] = _rms(x, g_ref[...]).astype(BF16)


def _proj_res_norm(a, w, layer, res, g, tm=512):
    m, k = a.shape
    n = w.shape[2]
    tm = min(tm, m)
    row = lambda i: (i, 0)
    fixed = lambda i: (0, 0)
    return pl.pallas_call(
        _proj_res_norm_kernel,
        grid=(m // tm,),
        in_specs=[pl.BlockSpec((tm, k), row),
                  pl.BlockSpec((None, k, n), lambda i: (layer, 0, 0),
                               pipeline_mode=pl.Buffered(1)),
                  pl.BlockSpec((tm, n), row),
                  pl.BlockSpec((1, n), fixed)],
        out_specs=[pl.BlockSpec((tm, n), row), pl.BlockSpec((tm, n), row)],
        out_shape=[jax.ShapeDtypeStruct((m, n), F32),
                   jax.ShapeDtypeStruct((m, n), BF16)],
        scratch_shapes=[pltpu.VMEM((k, n), BF16)],
        compiler_params=_params(1), name="proj_res_norm",
    )(a, w, res, g.reshape(1, n))


def _mem_kv_kernel(mem_ref, g_ref, w_ref, o_ref):
    hm = _rms(mem_ref[...], g_ref[0]).astype(BF16)
    half = w_ref.shape[2] // 2
    for c in range(2):
        w = w_ref[0, :, c * half:(c + 1) * half].astype(BF16)
        o_ref[0, :, c * half:(c + 1) * half] = jnp.dot(
            hm, w, preferred_element_type=F32).astype(BF16)


def _mem_kv(mem, g, w):
    depth, d, n = w.shape
    nm = mem.shape[0]
    return pl.pallas_call(
        _mem_kv_kernel,
        grid=(depth,),
        in_specs=[pl.BlockSpec((nm, d), lambda l: (0, 0)),
                  pl.BlockSpec((1, 1, d), lambda l: (l, 0, 0)),
                  pl.BlockSpec((1, d, n), lambda l: (l, 0, 0))],
        out_specs=pl.BlockSpec((1, nm, n), lambda l: (l, 0, 0)),
        out_shape=jax.ShapeDtypeStruct((depth, nm, n), BF16),
        compiler_params=_params(1), name="mem_kv",
    )(mem, g.reshape(depth, 1, d), w)


def _mem_attn_kernel(h_ref, x_ref, wq_ref, kv_ref, wo_ref, g_ref,
                     xo_ref, ho_ref, wqb_ref, wob_ref):
    @pl.when(pl.program_id(0) == 0)
    def _():
        _cast_rows(wqb_ref, wq_ref, 256)
        _cast_rows(wob_ref, wo_ref, 256)

    hd = MEM_HEAD_DIM
    nk = MEM_HEADS * hd
    nm = kv_ref.shape[1]
    heads = range(MEM_HEADS)
    scale = LOG2E / math.sqrt(hd)
    q = (jnp.dot(h_ref[...], wqb_ref[...], preferred_element_type=F32) * scale).astype(BF16)
    s = [lax.dot_general(q[:, h * hd:(h + 1) * hd], kv_ref[0, :, h * hd:(h + 1) * hd],
                         (((1,), (1,)), ((), ())), preferred_element_type=F32)
         for h in heads]
    p = [jnp.exp2(s[h] - jnp.max(s[h], axis=1, keepdims=True)).astype(BF16) for h in heads]
    ones = jnp.ones((nm, hd), BF16)
    o = []
    for h in heads:
        v1 = jnp.concatenate([kv_ref[0, :, nk + h * hd:nk + (h + 1) * hd], ones], axis=1)
        ol = jnp.dot(p[h], v1, preferred_element_type=F32)
        o.append((ol[:, :hd] / ol[:, hd:]).astype(BF16))
    x = x_ref[...] + jnp.dot(jnp.concatenate(o, axis=1), wob_ref[...], preferred_element_type=F32)
    xo_ref[...] = x
    ho_ref[...] = _rms(x, g_ref[...]).astype(BF16)


def _mem_attn(h, x, wq, kv_all, layer, wo, g, tm=512):
    m, d = h.shape
    nq = wq.shape[2]
    tm = min(tm, m)
    row = lambda i: (i, 0)
    fixed = lambda i: (0, 0)
    at_layer = lambda i: (layer, 0, 0)
    nm, nkv = kv_all.shape[1:]
    return pl.pallas_call(
        _mem_attn_kernel,
        grid=(m // tm,),
        in_specs=[pl.BlockSpec((tm, d), row),
                  pl.BlockSpec((tm, d), row),
                  pl.BlockSpec((None, d, nq), at_layer, pipeline_mode=pl.Buffered(1)),
                  pl.BlockSpec((1, nm, nkv), at_layer),
                  pl.BlockSpec((None, nq, d), at_layer, pipeline_mode=pl.Buffered(1)),
                  pl.BlockSpec((1, d), fixed)],
        out_specs=[pl.BlockSpec((tm, d), row), pl.BlockSpec((tm, d), row)],
        out_shape=[jax.ShapeDtypeStruct((m, d), F32),
                   jax.ShapeDtypeStruct((m, d), BF16)],
        scratch_shapes=[pltpu.VMEM((d, nq), BF16), pltpu.VMEM((nq, d), BF16)],
        compiler_params=_params(1), name="mem_attn",
    )(h, x, wq, kv_all, wo, g.reshape(1, d))


def _conv3(u_ref, r0, nr, cols, cw, cb):
    a = SUBLANES + r0
    return (cb + cw[2:3, :] * u_ref[a:a + nr, cols] + cw[1:2, :] * u_ref[a - 1:a - 1 + nr, cols]
            + cw[0:1, :] * u_ref[a - 2:a - 2 + nr, cols])


def _silu_gate(gate, up):
    return gate * (1.0 / (1.0 + jnp.exp(-gate))) * up


def _ffn_in_kernel(h_ref, wg_ref, wu_ref, cwg_ref, cwu_ref, cbg_ref, cbu_ref,
                   o_ref, wgb_ref, wub_ref, raw_ref, *, tn, tc, ni, nt):
    hs = SUBLANES
    tm = h_ref.shape[0]
    t = pl.program_id(0)
    slot = t % 2
    i_cur = jnp.minimum(t, nt - 1) % ni
    cur = (raw_ref.at[slot, 0], raw_ref.at[slot, 1])
    prev = (raw_ref.at[1 - slot, 0], raw_ref.at[1 - slot, 1])

    @pl.when(t == 0)
    def _():
        for r in prev:
            r[...] = jnp.zeros(r.shape, F32)

    @pl.when(jnp.logical_and(i_cur == 0, t < nt))
    def _():
        _cast_rows(wgb_ref, wg_ref, 256)
        _cast_rows(wub_ref, wu_ref, 256)
        for r in cur:
            r[:hs, :] = jnp.zeros((hs, tn), F32)

    @pl.when(i_cur > 0)
    def _():
        for r, p in zip(cur, prev):
            r[:hs, :] = p[tm:tm + hs, :]

    def epilogue(r0, nr):
        tok = jnp.zeros((hs, LANES), F32)
        for c in range(tn // tc):
            cols = slice(c * tc, (c + 1) * tc)
            gate = _conv3(prev[0], r0, nr, cols, cwg_ref[:, cols], cbg_ref[:, cols])
            up = _conv3(prev[1], r0, nr, cols, cwu_ref[:, cols], cbu_ref[:, cols])
            g = _silu_gate(gate, up)
            o_ref[r0:r0 + nr, cols] = g.astype(o_ref.dtype)
            part = jnp.sum(g.reshape(nr // hs, hs, tc), axis=0)
            for l in range(tc // LANES):
                tok = tok + part[:, l * LANES:(l + 1) * LANES]
        return tok

    h = h_ref[...]
    q = tm // 4
    tok = epilogue(0, q)
    acc = jnp.dot(h, wgb_ref[...], preferred_element_type=F32)
    cur[0][hs:, :] = acc
    zero = (pltpu.bitcast(tok, jnp.uint32) >> 16) >> 16
    tile = pltpu.bitcast(acc[:hs, :LANES], jnp.uint32) | zero
    cur[0][hs:2 * hs, :LANES] = pltpu.bitcast(tile, F32)
    epilogue(q, tm - q)
    cur[1][hs:, :] = jnp.dot(h, wub_ref[...], preferred_element_type=F32)


def _ffn_in(h, w_in, conv_w, conv_b, layer, tm=1024, tn=512, tc=256):
    m, k = h.shape
    dff = w_in.shape[2] // 2
    tm = min(tm, m)
    nj, ni = dff // tn, m // tm
    nt = nj * ni
    conv_b = conv_b.reshape(conv_b.shape[0], 1, -1)
    mm_tile = lambda t: jnp.minimum(t, nt - 1)
    ep_tile = lambda t: jnp.maximum(t - 1, 0)
    gate_w = lambda t: (layer, 0, mm_tile(t) // ni)
    up_w = lambda t: (layer, 0, nj + mm_tile(t) // ni)
    gate_c = lambda t: (layer, 0, ep_tile(t) // ni)
    up_c = lambda t: (layer, 0, nj + ep_tile(t) // ni)
    return pl.pallas_call(
        functools.partial(_ffn_in_kernel, tn=tn, tc=tc, ni=ni, nt=nt),
        grid=(nt + 1,),
        in_specs=[pl.BlockSpec((tm, k), lambda t: (mm_tile(t) % ni, 0)),
                  pl.BlockSpec((None, k, tn), gate_w),
                  pl.BlockSpec((None, k, tn), up_w),
                  pl.BlockSpec((None, CONV_W, tn), gate_c),
                  pl.BlockSpec((None, CONV_W, tn), up_c),
                  pl.BlockSpec((None, 1, tn), gate_c),
                  pl.BlockSpec((None, 1, tn), up_c)],
        out_specs=pl.BlockSpec((tm, tn), lambda t: (ep_tile(t) % ni, ep_tile(t) // ni)),
        out_shape=jax.ShapeDtypeStruct((m, dff), BF16),
        scratch_shapes=[pltpu.VMEM((k, tn), BF16), pltpu.VMEM((k, tn), BF16),
                        pltpu.VMEM((2, 2, SUBLANES + tm, tn), F32)],
        compiler_params=_params(1), name="ffn_in",
    )(h, w_in, w_in, conv_w, conv_w, conv_b, conv_b)


def _rope_slots(t):
    half = ROPE_DIM // 2
    z = jnp.zeros(t.shape[:-1] + (LANES // 2 - half,), t.dtype)
    return jnp.concatenate([t[..., :half], z, t[..., half:], z], axis=-1)


def _rope_tables(positions):
    inv_freq = ROPE_THETA ** (-jnp.arange(0, ROPE_DIM, 2, dtype=F32) / ROPE_DIM)
    ang = positions.astype(F32)[:, None] * inv_freq
    cos, sin = jnp.cos(ang), jnp.sin(ang)
    return (_rope_slots(jnp.concatenate([cos, cos], axis=-1)),
            _rope_slots(jnp.concatenate([-sin, sin], axis=-1)))


def kernel(x, mem, positions, norm_mix, norm_mem_q, norm_mem_kv, norm_ffn, norm_final, mla_w_down, mla_q_norm, mla_w_uq, mla_kv_norm, mla_w_ukv, mla_w_o, sb_w_qkv, sb_w_o, mem_w_q, mem_w_kv, mem_w_o, ffn_w_in, ffn_conv_w, ffn_conv_b, ffn_w_out):
    assert x.shape[0] == 1, "batch is fixed to 1"
    xs = x[0]
    cos, sin = _rope_tables(positions[0])
    mem_kv = _mem_kv(mem[0], norm_mem_kv, mem_w_kv)
    mla_scale = LOG2E / math.sqrt(NOPE_DIM + ROPE_DIM)
    sb_scale = 1.0 / math.sqrt(SB_HEAD_DIM)

    h = _rmsnorm(xs, norm_mix[0], BF16)
    for i in range(DEPTH):
        j = i // N_MIXERS
        if i % N_MIXERS == 0:
            nlat = Q_LORA + KV_LORA
            w_down = jnp.concatenate(
                [mla_w_down[j][:, :nlat], _rope_slots(mla_w_down[j][:, nlat:])], axis=1)
            c_q, c_kv, k_rope = _mla_down(h, w_down, mla_q_norm[j], mla_kv_norm[j], cos, sin)
            w_uq = mla_w_uq[j].reshape(Q_LORA, MLA_HEADS, NOPE_DIM + ROPE_DIM)
            w_qn = w_uq[:, :, :NOPE_DIM].reshape(Q_LORA, MLA_HEADS * NOPE_DIM)
            w_qr = _rope_slots(w_uq[:, :, NOPE_DIM:]).reshape(Q_LORA, MLA_HEADS * LANES)
            qn = _mm(c_q, w_qn[None], 0, tm=1024, tn=1024, out_dtype=BF16, mode="scale",
                     scale=mla_scale)
            qr = _mm(c_q, w_qr[None], 0, tm=1024, tn=1024, out_dtype=BF16, mode="rope",
                     scale=mla_scale, cos=cos, sin=sin)
            kv = _mm(c_kv, mla_w_ukv, j, tm=1024, tn=1024, out_dtype=BF16)
            o = _mla_attn(qn, qr, kv, k_rope)
            w_o = mla_w_o
        else:
            qkv = _mm(h, sb_w_qkv, j, tm=1024, tn=1024, out_dtype=BF16, mode="scale_cols",
                      scale=sb_scale, scale_cols=SB_HEADS * SB_HEAD_DIM)
            o = _sb_attn(qkv)
            w_o = sb_w_o
        xs, hq = _proj_res_norm(o, w_o, j, xs, norm_mem_q[i])
        xs, hf = _mem_attn(hq, xs, mem_w_q, mem_kv, i, mem_w_o, norm_ffn[i])
        g = _ffn_in(hf, ffn_w_in, ffn_conv_w, ffn_conv_b, i)
        xs = _mm(g, ffn_w_out, i, tm=512, tn=512, out_dtype=F32, mode="residual", res=xs)
        last = i == DEPTH - 1
        h = _rmsnorm(xs, norm_final if last else norm_mix[i + 1], F32 if last else BF16)
    return h[None]
```

```python
import functools
import math

import jax
import jax.numpy as jnp
from jax import lax
from jax.experimental import pallas as pl
from jax.experimental.pallas import tpu as pltpu

F32 = jnp.float32
BF16 = jnp.bfloat16

D_MODEL = 2048
DEPTH = 4
CHUNK = 64
N_MIXERS = 2
MLA_HEADS = 16
Q_LORA = 512
KV_LORA = 512
NOPE_DIM = 128
ROPE_DIM = 64
V_DIM = 128
ROPE_THETA = 10000.0
SB_HEADS = 16
SB_HEAD_DIM = 128
N_MEM = 256
MEM_HEADS = 4
MEM_HEAD_DIM = 128
D_FF = 5632
CONV_W = 3
EPS = 1e-6

LANES = 128
SUBLANES = 8
VMEM_LIMIT = 56 * 1024 * 1024
NEG_BIG = -1e30
LOG2E = 1.4426950408889634
SB_STOP = 120.0


def _params(n_grid, flags=None):
    return pltpu.CompilerParams(
        dimension_semantics=("arbitrary",) * n_grid,
        vmem_limit_bytes=VMEM_LIMIT, flags=flags)


def _cast_rows(dst_ref, src_ref, rows):
    n = src_ref.shape[0]
    rows = min(rows, n)

    def body(r, c):
        sl = pl.ds(pl.multiple_of(r * rows, rows), rows)
        dst_ref[sl, :] = src_ref[sl, :].astype(dst_ref.dtype)
        return c

    lax.fori_loop(0, n // rows, body, 0)


def _rms(x, g):
    ms = jnp.mean(x * x, axis=-1, keepdims=True)
    return x * lax.rsqrt(ms + EPS) * g


def _rmsnorm_kernel(x_ref, g_ref, o_ref):
    o_ref[...] = _rms(x_ref[...], g_ref[...]).astype(o_ref.dtype)


def _rmsnorm(x, g, out_dtype, tm=512):
    m, d = x.shape
    tm = min(tm, m)
    return pl.pallas_call(
        _rmsnorm_kernel,
        grid=(m // tm,),
        in_specs=[pl.BlockSpec((tm, d), lambda i: (i, 0)),
                  pl.BlockSpec((1, d), lambda i: (0, 0))],
        out_specs=pl.BlockSpec((tm, d), lambda i: (i, 0)),
        out_shape=jax.ShapeDtypeStruct((m, d), out_dtype),
        compiler_params=_params(1), name="rmsnorm",
    )(x, g.reshape(1, d))


def _rope_heads(acc, cos, sin):
    outs = []
    for h in range(acc.shape[1] // LANES):
        a = acc[:, h * LANES:(h + 1) * LANES]
        outs.append(a * cos + pltpu.roll(a, LANES // 2, axis=1) * sin)
    return outs[0] if len(outs) == 1 else jnp.concatenate(outs, axis=1)


def _mm_kernel(*refs, mode, scale, scale_cols, tn):
    if mode == "rope":
        x_ref, w_ref, cos_ref, sin_ref, o_ref, wb_ref = refs
    elif mode == "residual":
        x_ref, w_ref, res_ref, o_ref, wb_ref = refs
    else:
        x_ref, w_ref, o_ref, wb_ref = refs

    @pl.when(pl.program_id(1) == 0)
    def _():
        _cast_rows(wb_ref, w_ref, 256)

    acc = jnp.dot(x_ref[...], wb_ref[...], preferred_element_type=F32)
    if mode == "rope":
        acc = _rope_heads(acc, cos_ref[...], sin_ref[...]) * scale
    elif mode == "residual":
        acc = acc + res_ref[...]
    elif mode == "scale":
        acc = acc * scale
    elif mode == "scale_cols":
        s = jnp.where(pl.program_id(0) * tn < scale_cols, scale, 1.0).astype(F32)
        acc = acc * s
    o_ref[...] = acc.astype(o_ref.dtype)


def _mm(x, w, layer, *, tm, tn, out_dtype, mode="plain", scale=1.0, scale_cols=0,
        cos=None, sin=None, res=None):
    m, k = x.shape
    n = w.shape[2]
    tm, tn = min(tm, m), min(tn, n)
    in_specs = [pl.BlockSpec((tm, k), lambda j, i: (i, 0)),
                pl.BlockSpec((None, k, tn), lambda j, i: (layer, 0, j))]
    args = [x, w]
    if mode == "rope":
        in_specs += [pl.BlockSpec((tm, LANES), lambda j, i: (i, 0))] * 2
        args += [cos, sin]
    elif mode == "residual":
        in_specs += [pl.BlockSpec((tm, tn), lambda j, i: (i, j))]
        args += [res]
    return pl.pallas_call(
        functools.partial(_mm_kernel, mode=mode, scale=scale, scale_cols=scale_cols, tn=tn),
        grid=(n // tn, m // tm),
        in_specs=in_specs,
        out_specs=pl.BlockSpec((tm, tn), lambda j, i: (i, j)),
        out_shape=jax.ShapeDtypeStruct((m, n), out_dtype),
        scratch_shapes=[pltpu.VMEM((k, tn), BF16)],
        compiler_params=_params(2), name="mm_" + mode,
    )(*args)


def _mla_down_kernel(h_ref, w_ref, gq_ref, gkv_ref, cos_ref, sin_ref,
                     cq_ref, ckv_ref, kr_ref, wb_ref):
    @pl.when(pl.program_id(0) == 0)
    def _():
        _cast_rows(wb_ref, w_ref, 256)

    acc = jnp.dot(h_ref[...], wb_ref[...], preferred_element_type=F32)
    cq_ref[...] = _rms(acc[:, :Q_LORA], gq_ref[...]).astype(BF16)
    ckv_ref[...] = _rms(acc[:, Q_LORA:Q_LORA + KV_LORA], gkv_ref[...]).astype(BF16)
    kr = acc[:, Q_LORA + KV_LORA:]
    kr_ref[...] = _rope_heads(kr, cos_ref[...], sin_ref[...]).astype(BF16)


def _mla_down(h, w, gq, gkv, cos, sin, tm=512):
    m, k = h.shape
    n = w.shape[1]
    tm = min(tm, m)
    row = lambda i: (i, 0)
    fixed = lambda i: (0, 0)
    return pl.pallas_call(
        _mla_down_kernel,
        grid=(m // tm,),
        in_specs=[pl.BlockSpec((tm, k), row),
                  pl.BlockSpec((k, n), fixed),
                  pl.BlockSpec((1, Q_LORA), fixed),
                  pl.BlockSpec((1, KV_LORA), fixed),
                  pl.BlockSpec((tm, LANES), row),
                  pl.BlockSpec((tm, LANES), row)],
        out_specs=[pl.BlockSpec((tm, Q_LORA), row),
                   pl.BlockSpec((tm, KV_LORA), row),
                   pl.BlockSpec((tm, LANES), row)],
        out_shape=[jax.ShapeDtypeStruct((m, Q_LORA), BF16),
                   jax.ShapeDtypeStruct((m, KV_LORA), BF16),
                   jax.ShapeDtypeStruct((m, LANES), BF16)],
        scratch_shapes=[pltpu.VMEM((k, n), BF16)],
        compiler_params=_params(1), name="mla_down",
    )(h, w, gq.reshape(1, -1), gkv.reshape(1, -1), cos, sin)


def _mla_attn_kernel(qn_ref, qr_ref, kv_ref, kr_ref, o_ref,
                     s_ref, p_ref, m_ref, l_ref, acc_ref, *, tq, nh, rg):
    i = pl.program_id(1)
    hd = NOPE_DIM
    m_ref[...] = jnp.full(m_ref.shape, NEG_BIG, F32)
    l_ref[...] = jnp.zeros_like(l_ref)
    acc_ref[...] = jnp.zeros_like(acc_ref)

    def step(kb, masked):
        rows = pl.ds(pl.multiple_of(kb * tq, tq), tq)
        kr = kr_ref[rows, :]
        for h in range(nh):
            q = jnp.concatenate([qn_ref[:, h * hd:(h + 1) * hd],
                                 qr_ref[:, h * hd:(h + 1) * hd]], axis=1)
            k = jnp.concatenate([kv_ref[rows, 2 * h * hd:(2 * h + 1) * hd], kr], axis=1)
            s = lax.dot_general(q, k, (((1,), (1,)), ((), ())),
                                preferred_element_type=F32)
            if masked:
                qc = lax.broadcasted_iota(jnp.int32, (tq, tq), 0) // CHUNK
                kc = lax.broadcasted_iota(jnp.int32, (tq, tq), 1) // CHUNK
                s = jnp.where(kc <= qc, s, NEG_BIG)
            s_ref[h] = s
        nt = tq // LANES
        alphas = []
        for h in range(nh):
            m_tile = s_ref[h, :, 0:LANES]
            for c in range(1, nt):
                m_tile = jnp.maximum(m_tile, s_ref[h, :, c * LANES:(c + 1) * LANES])
            m_prev = m_ref[h]
            m_rep = jnp.maximum(m_prev, jnp.max(m_tile, axis=1, keepdims=True))
            m_ref[h] = m_rep
            alpha = jnp.exp2(m_prev - m_rep)
            alphas.append(alpha)
            for r in range(tq // rg):
                rs = slice(r * rg, (r + 1) * rg)
                part = None
                for c in range(nt):
                    cs = slice(c * LANES, (c + 1) * LANES)
                    p = jnp.exp2(s_ref[h, rs, cs] - m_rep[rs])
                    p_ref[h, rs, cs] = p.astype(BF16)
                    part = p if part is None else part + p
                l_ref[h, rs, :] = alpha[rs] * l_ref[h, rs, :] + part
        for h in range(nh):
            v = kv_ref[rows, (2 * h + 1) * hd:(2 * h + 2) * hd]
            acc_ref[h] = alphas[h] * acc_ref[h] + jnp.dot(p_ref[h], v, preferred_element_type=F32)

    def body(kb, c):
        step(kb, False)
        return c

    lax.fori_loop(0, i, body, 0)
    step(i, True)
    for h in range(nh):
        l = jnp.sum(l_ref[h], axis=1, keepdims=True)
        o_ref[:, h * hd:(h + 1) * hd] = (acc_ref[h] / l).astype(o_ref.dtype)


def _mla_attn(qn, qr, kv, kr, tq=512, nh=4, rg=32):
    s = qn.shape[0]
    tq = min(tq, s)
    assert CHUNK % rg == 0 and tq % CHUNK == 0
    return pl.pallas_call(
        functools.partial(_mla_attn_kernel, tq=tq, nh=nh, rg=rg),
        scratch_shapes=[pltpu.VMEM((nh, tq, tq), F32), pltpu.VMEM((nh, tq, tq), BF16),
                        pltpu.VMEM((nh, tq, LANES), F32), pltpu.VMEM((nh, tq, LANES), F32),
                        pltpu.VMEM((nh, tq, V_DIM), F32)],
        grid=(MLA_HEADS // nh, s // tq),
        in_specs=[pl.BlockSpec((tq, nh * NOPE_DIM), lambda g, i: (i, g)),
                  pl.BlockSpec((tq, nh * LANES), lambda g, i: (i, g)),
                  pl.BlockSpec((s, nh * (NOPE_DIM + V_DIM)), lambda g, i: (0, g)),
                  pl.BlockSpec((s, LANES), lambda g, i: (0, 0))],
        out_specs=pl.BlockSpec((tq, nh * V_DIM), lambda g, i: (i, g)),
        out_shape=jax.ShapeDtypeStruct((s, MLA_HEADS * V_DIM), BF16),
        compiler_params=_params(2), name="mla_attn",
    )(qn, qr, kv, kr)


def _sb_attn_kernel(q_ref, k_ref, v_ref, o_ref, carry_ref, acc_ref, *, tq, nh):
    i = pl.program_id(1)
    hd = SB_HEAD_DIM
    row = lax.broadcasted_iota(jnp.int32, (tq, tq), 0)
    col = lax.broadcasted_iota(jnp.int32, (tq, tq), 1)
    upper = jnp.where(row > col, 1.0, 0.0).astype(BF16)
    carry_ref[...] = jnp.zeros_like(carry_ref)
    acc_ref[...] = jnp.zeros_like(acc_ref)

    def step(kb, diagonal):
        rows = pl.ds(pl.multiple_of(kb * tq, tq), tq)
        heads = range(nh)
        hcols = [slice(h * hd, (h + 1) * hd) for h in heads]
        z = [lax.dot_general(q_ref[:, hcols[h]], k_ref[rows, hcols[h]], (((1,), (1,)), ((), ())),
                             preferred_element_type=F32) for h in heads]
        sp = [jnp.maximum(z[h], 0.0) + jnp.log(1.0 + jnp.exp(-jnp.abs(z[h]))) for h in heads]
        lk = [jnp.where(col < row, -sp[h], 0.0) if diagonal else -sp[h] for h in heads]
        later = []
        for h in heads:
            lk_hi = lk[h].astype(BF16)
            lk_lo = (lk[h] - lk_hi.astype(F32)).astype(BF16)
            later.append(jnp.dot(lk_hi, upper, preferred_element_type=F32)
                         + jnp.dot(lk_lo, upper, preferred_element_type=F32))
        top = None
        for h in heads:
            carry = carry_ref[h]
            a = jnp.exp(z[h] - sp[h] + later[h] + jnp.tile(carry, (1, tq // LANES)))
            if diagonal:
                a = jnp.where(col < row, a, 0.0)
            acc_ref[:, hcols[h]] += jnp.dot(a.astype(BF16), v_ref[rows, hcols[h]],
                                            preferred_element_type=F32)
            carry = carry + jnp.sum(lk[h], axis=1, keepdims=True)
            carry_ref[h] = carry
            top = carry if top is None else jnp.maximum(top, carry)
        return jnp.max(top) > -SB_STOP

    def cond(state):
        kb, live = state
        return jnp.logical_and(kb >= 0, live)

    def body(state):
        kb, _ = state
        return kb - 1, step(kb, False)

    lax.while_loop(cond, body, (i - 1, step(i, True)))
    o_ref[...] = acc_ref[...].astype(o_ref.dtype)


def _sb_attn(qkv, tq=256, nh=4):
    s = qkv.shape[0]
    tq = min(tq, s)
    ng = SB_HEADS // nh
    w = nh * SB_HEAD_DIM
    return pl.pallas_call(
        functools.partial(_sb_attn_kernel, tq=tq, nh=nh),
        grid=(ng, s // tq),
        in_specs=[pl.BlockSpec((tq, w), lambda g, i: (i, g)),
                  pl.BlockSpec((s, w), lambda g, i: (0, ng + g)),
                  pl.BlockSpec((s, w), lambda g, i: (0, 2 * ng + g))],
        out_specs=pl.BlockSpec((tq, w), lambda g, i: (i, g)),
        out_shape=jax.ShapeDtypeStruct((s, SB_HEADS * SB_HEAD_DIM), BF16),
        scratch_shapes=[pltpu.VMEM((nh, tq, LANES), F32), pltpu.VMEM((tq, w), F32)],
        compiler_params=_params(2), name="sb_attn",
    )(qkv, qkv, qkv)


def _proj_res_norm_kernel(a_ref, w_ref, res_ref, g_ref, x_ref, h_ref, wb_ref):
    @pl.when(pl.program_id(0) == 0)
    def _():
        _cast_rows(wb_ref, w_ref, 256)

    x = res_ref[...] + jnp.dot(a_ref[...], wb_ref[...], preferred_element_type=F32)
    x_ref[...] = x
    h_ref[...] = _rms(x, g_ref[...]).astype(BF16)


def _proj_res_norm(a, w, layer, res, g, tm=512):
    m, k = a.shape
    n = w.shape[2]
    tm = min(tm, m)
    row = lambda i: (i, 0)
    fixed = lambda i: (0, 0)
    return pl.pallas_call(
        _proj_res_norm_kernel,
        grid=(m // tm,),
        in_specs=[pl.BlockSpec((tm, k), row),
                  pl.BlockSpec((None, k, n), lambda i: (layer, 0, 0),
                               pipeline_mode=pl.Buffered(1)),
                  pl.BlockSpec((tm, n), row),
                  pl.BlockSpec((1, n), fixed)],
        out_specs=[pl.BlockSpec((tm, n), row), pl.BlockSpec((tm, n), row)],
        out_shape=[jax.ShapeDtypeStruct((m, n), F32),
                   jax.ShapeDtypeStruct((m, n), BF16)],
        scratch_shapes=[pltpu.VMEM((k, n), BF16)],
        compiler_params=_params(1), name="proj_res_norm",
    )(a, w, res, g.reshape(1, n))


def _mem_kv_kernel(mem_ref, g_ref, w_ref, o_ref):
    hm = _rms(mem_ref[...], g_ref[0]).astype(BF16)
    half = w_ref.shape[2] // 2
    for c in range(2):
        w = w_ref[0, :, c * half:(c + 1) * half].astype(BF16)
        o_ref[0, :, c * half:(c + 1) * half] = jnp.dot(
            hm, w, preferred_element_type=F32).astype(BF16)


def _mem_kv(mem, g, w):
    depth, d, n = w.shape
    nm = mem.shape[0]
    return pl.pallas_call(
        _mem_kv_kernel,
        grid=(depth,),
        in_specs=[pl.BlockSpec((nm, d), lambda l: (0, 0)),
                  pl.BlockSpec((1, 1, d), lambda l: (l, 0, 0)),
                  pl.BlockSpec((1, d, n), lambda l: (l, 0, 0))],
        out_specs=pl.BlockSpec((1, nm, n), lambda l: (l, 0, 0)),
        out_shape=jax.ShapeDtypeStruct((depth, nm, n), BF16),
        compiler_params=_params(1), name="mem_kv",
    )(mem, g.reshape(depth, 1, d), w)


def _mem_attn_kernel(h_ref, x_ref, wq_ref, kv_ref, wo_ref, g_ref,
                     xo_ref, ho_ref, wqb_ref, wob_ref):
    @pl.when(pl.program_id(0) == 0)
    def _():
        _cast_rows(wqb_ref, wq_ref, 256)
        _cast_rows(wob_ref, wo_ref, 256)

    hd = MEM_HEAD_DIM
    nk = MEM_HEADS * hd
    nm = kv_ref.shape[1]
    heads = range(MEM_HEADS)
    scale = LOG2E / math.sqrt(hd)
    q = (jnp.dot(h_ref[...], wqb_ref[...], preferred_element_type=F32) * scale).astype(BF16)
    s = [lax.dot_general(q[:, h * hd:(h + 1) * hd], kv_ref[0, :, h * hd:(h + 1) * hd],
                         (((1,), (1,)), ((), ())), preferred_element_type=F32)
         for h in heads]
    p = [jnp.exp2(s[h] - jnp.max(s[h], axis=1, keepdims=True)).astype(BF16) for h in heads]
    ones = jnp.ones((nm, hd), BF16)
    o = []
    for h in heads:
        v1 = jnp.concatenate([kv_ref[0, :, nk + h * hd:nk + (h + 1) * hd], ones], axis=1)
        ol = jnp.dot(p[h], v1, preferred_element_type=F32)
        o.append((ol[:, :hd] / ol[:, hd:]).astype(BF16))
    x = x_ref[...] + jnp.dot(jnp.concatenate(o, axis=1), wob_ref[...], preferred_element_type=F32)
    xo_ref[...] = x
    ho_ref[...] = _rms(x, g_ref[...]).astype(BF16)


def _mem_attn(h, x, wq, kv_all, layer, wo, g, tm=512):
    m, d = h.shape
    nq = wq.shape[2]
    tm = min(tm, m)
    row = lambda i: (i, 0)
    fixed = lambda i: (0, 0)
    at_layer = lambda i: (layer, 0, 0)
    nm, nkv = kv_all.shape[1:]
    return pl.pallas_call(
        _mem_attn_kernel,
        grid=(m // tm,),
        in_specs=[pl.BlockSpec((tm, d), row),
                  pl.BlockSpec((tm, d), row),
                  pl.BlockSpec((None, d, nq), at_layer, pipeline_mode=pl.Buffered(1)),
                  pl.BlockSpec((1, nm, nkv), at_layer),
                  pl.BlockSpec((None, nq, d), at_layer, pipeline_mode=pl.Buffered(1)),
                  pl.BlockSpec((1, d), fixed)],
        out_specs=[pl.BlockSpec((tm, d), row), pl.BlockSpec((tm, d), row)],
        out_shape=[jax.ShapeDtypeStruct((m, d), F32),
                   jax.ShapeDtypeStruct((m, d), BF16)],
        scratch_shapes=[pltpu.VMEM((d, nq), BF16), pltpu.VMEM((nq, d), BF16)],
        compiler_params=_params(1), name="mem_attn",
    )(h, x, wq, kv_all, wo, g.reshape(1, d))


def _conv3(u_ref, r0, nr, cols, cw, cb):
    a = SUBLANES + r0
    return (cb + cw[2:3, :] * u_ref[a:a + nr, cols] + cw[1:2, :] * u_ref[a - 1:a - 1 + nr, cols]
            + cw[0:1, :] * u_ref[a - 2:a - 2 + nr, cols])


def _silu_gate(gate, up):
    return gate * (1.0 / (1.0 + jnp.exp(-gate))) * up


def _ffn_in_kernel(h_ref, wg_ref, wu_ref, cwg_ref, cwu_ref, cbg_ref, cbu_ref,
                   o_ref, wgb_ref, wub_ref, raw_ref, *, tn, tc, ni, nt):
    hs = SUBLANES
    tm = h_ref.shape[0]
    t = pl.program_id(0)
    slot = t % 2
    i_cur = jnp.minimum(t, nt - 1) % ni
    cur = (raw_ref.at[slot, 0], raw_ref.at[slot, 1])
    prev = (raw_ref.at[1 - slot, 0], raw_ref.at[1 - slot, 1])

    @pl.when(t == 0)
    def _():
        for r in prev:
            r[...] = jnp.zeros(r.shape, F32)

    @pl.when(jnp.logical_and(i_cur == 0, t < nt))
    def _():
        _cast_rows(wgb_ref, wg_ref, 256)
        _cast_rows(wub_ref, wu_ref, 256)
        for r in cur:
            r[:hs, :] = jnp.zeros((hs, tn), F32)

    @pl.when(i_cur > 0)
    def _():
        for r, p in zip(cur, prev):
            r[:hs, :] = p[tm:tm + hs, :]

    def epilogue(r0, nr):
        tok = jnp.zeros((hs, LANES), F32)
        for c in range(tn // tc):
            cols = slice(c * tc, (c + 1) * tc)
            gate = _conv3(prev[0], r0, nr, cols, cwg_ref[:, cols], cbg_ref[:, cols])
            up = _conv3(prev[1], r0, nr, cols, cwu_ref[:, cols], cbu_ref[:, cols])
            g = _silu_gate(gate, up)
            o_ref[r0:r0 + nr, cols] = g.astype(o_ref.dtype)
            part = jnp.sum(g.reshape(nr // hs, hs, tc), axis=0)
            for l in range(tc // LANES):
                tok = tok + part[:, l * LANES:(l + 1) * LANES]
        return tok

    h = h_ref[...]
    q = tm // 4
    tok = epilogue(0, q)
    acc = jnp.dot(h, wgb_ref[...], preferred_element_type=F32)
    cur[0][hs:, :] = acc
    zero = (pltpu.bitcast(tok, jnp.uint32) >> 16) >> 16
    cur[0][hs:2 * hs, :LANES] = jnp.where(zero == 0, acc[:hs, :LANES], acc[hs:2 * hs, :LANES])
    epilogue(q, tm - q)
    cur[1][hs:, :] = jnp.dot(h, wub_ref[...], preferred_element_type=F32)


def _ffn_in(h, w_in, conv_w, conv_b, layer, tm=1024, tn=512, tc=256):
    m, k = h.shape
    dff = w_in.shape[2] // 2
    tm = min(tm, m)
    nj, ni = dff // tn, m // tm
    nt = nj * ni
    conv_b = conv_b.reshape(conv_b.shape[0], 1, -1)
    mm_tile = lambda t: jnp.minimum(t, nt - 1)
    ep_tile = lambda t: jnp.maximum(t - 1, 0)
    gate_w = lambda t: (layer, 0, mm_tile(t) // ni)
    up_w = lambda t: (layer, 0, nj + mm_tile(t) // ni)
    gate_c = lambda t: (layer, 0, ep_tile(t) // ni)
    up_c = lambda t: (layer, 0, nj + ep_tile(t) // ni)
    return pl.pallas_call(
        functools.partial(_ffn_in_kernel, tn=tn, tc=tc, ni=ni, nt=nt),
        grid=(nt + 1,),
        in_specs=[pl.BlockSpec((tm, k), lambda t: (mm_tile(t) % ni, 0)),
                  pl.BlockSpec((None, k, tn), gate_w),
                  pl.BlockSpec((None, k, tn), up_w),
                  pl.BlockSpec((None, CONV_W, tn), gate_c),
                  pl.BlockSpec((None, CONV_W, tn), up_c),
                  pl.BlockSpec((None, 1, tn), gate_c),
                  pl.BlockSpec((None, 1, tn), up_c)],
        out_specs=pl.BlockSpec((tm, tn), lambda t: (ep_tile(t) % ni, ep_tile(t) // ni)),
        out_shape=jax.ShapeDtypeStruct((m, dff), BF16),
        scratch_shapes=[pltpu.VMEM((k, tn), BF16), pltpu.VMEM((k, tn), BF16),
                        pltpu.VMEM((2, 2, SUBLANES + tm, tn), F32)],
        compiler_params=_params(1), name="ffn_in",
    )(h, w_in, w_in, conv_w, conv_w, conv_b, conv_b)


def _rope_slots(t):
    half = ROPE_DIM // 2
    z = jnp.zeros(t.shape[:-1] + (LANES // 2 - half,), t.dtype)
    return jnp.concatenate([t[..., :half], z, t[..., half:], z], axis=-1)


def _rope_tables(positions):
    inv_freq = ROPE_THETA ** (-jnp.arange(0, ROPE_DIM, 2, dtype=F32) / ROPE_DIM)
    ang = positions.astype(F32)[:, None] * inv_freq
    cos, sin = jnp.cos(ang), jnp.sin(ang)
    return (_rope_slots(jnp.concatenate([cos, cos], axis=-1)),
            _rope_slots(jnp.concatenate([-sin, sin], axis=-1)))


def kernel(x, mem, positions, norm_mix, norm_mem_q, norm_mem_kv, norm_ffn, norm_final, mla_w_down, mla_q_norm, mla_w_uq, mla_kv_norm, mla_w_ukv, mla_w_o, sb_w_qkv, sb_w_o, mem_w_q, mem_w_kv, mem_w_o, ffn_w_in, ffn_conv_w, ffn_conv_b, ffn_w_out):
    assert x.shape[0] == 1, "batch is fixed to 1"
    xs = x[0]
    cos, sin = _rope_tables(positions[0])
    mem_kv = _mem_kv(mem[0], norm_mem_kv, mem_w_kv)
    mla_scale = LOG2E / math.sqrt(NOPE_DIM + ROPE_DIM)
    sb_scale = 1.0 / math.sqrt(SB_HEAD_DIM)

    h = _rmsnorm(xs, norm_mix[0], BF16)
    for i in range(DEPTH):
        j = i // N_MIXERS
        if i % N_MIXERS == 0:
            nlat = Q_LORA + KV_LORA
            w_down = jnp.concatenate(
                [mla_w_down[j][:, :nlat], _rope_slots(mla_w_down[j][:, nlat:])], axis=1)
            c_q, c_kv, k_rope = _mla_down(h, w_down, mla_q_norm[j], mla_kv_norm[j], cos, sin)
            w_uq = mla_w_uq[j].reshape(Q_LORA, MLA_HEADS, NOPE_DIM + ROPE_DIM)
            w_qn = w_uq[:, :, :NOPE_DIM].reshape(Q_LORA, MLA_HEADS * NOPE_DIM)
            w_qr = _rope_slots(w_uq[:, :, NOPE_DIM:]).reshape(Q_LORA, MLA_HEADS * LANES)
            qn = _mm(c_q, w_qn[None], 0, tm=1024, tn=1024, out_dtype=BF16, mode="scale",
                     scale=mla_scale)
            qr = _mm(c_q, w_qr[None], 0, tm=1024, tn=1024, out_dtype=BF16, mode="rope",
                     scale=mla_scale, cos=cos, sin=sin)
            kv = _mm(c_kv, mla_w_ukv, j, tm=1024, tn=1024, out_dtype=BF16)
            o = _mla_attn(qn, qr, kv, k_rope)
            w_o = mla_w_o
        else:
            qkv = _mm(h, sb_w_qkv, j, tm=1024, tn=1024, out_dtype=BF16, mode="scale_cols",
                      scale=sb_scale, scale_cols=SB_HEADS * SB_HEAD_DIM)
            o = _sb_attn(qkv)
            w_o = sb_w_o
        xs, hq = _proj_res_norm(o, w_o, j, xs, norm_mem_q[i])
        xs, hf = _mem_attn(hq, xs, mem_w_q, mem_kv, i, mem_w_o, norm_ffn[i])
        g = _ffn_in(hf, ffn_w_in, ffn_conv_w, ffn_conv_b, i)
        xs = _mm(g, ffn_w_out, i, tm=512, tn=512, out_dtype=F32, mode="residual", res=xs)
        last = i == DEPTH - 1
        h = _rmsnorm(xs, norm_final if last else norm_mix[i + 1], F32 if last else BF16)
    return h[None]
```

```python
import functools
import math

import jax
import jax.numpy as jnp
from jax import lax
from jax.experimental import pallas as pl
from jax.experimental.pallas import tpu as pltpu

F32 = jnp.float32
BF16 = jnp.bfloat16

D_MODEL = 2048
DEPTH = 4
CHUNK = 64
N_MIXERS = 2
MLA_HEADS = 16
Q_LORA = 512
KV_LORA = 512
NOPE_DIM = 128
ROPE_DIM = 64
V_DIM = 128
ROPE_THETA = 10000.0
SB_HEADS = 16
SB_HEAD_DIM = 128
N_MEM = 256
MEM_HEADS = 4
MEM_HEAD_DIM = 128
D_FF = 5632
CONV_W = 3
EPS = 1e-6

LANES = 128
SUBLANES = 8
VMEM_LIMIT = 56 * 1024 * 1024
NEG_BIG = -1e30
LOG2E = 1.4426950408889634
SB_STOP = 120.0


def _params(n_grid, flags=None):
    return pltpu.CompilerParams(
        dimension_semantics=("arbitrary",) * n_grid,
        vmem_limit_bytes=VMEM_LIMIT, flags=flags)


def _cast_rows(dst_ref, src_ref, rows):
    n = src_ref.shape[0]
    rows = min(rows, n)

    def body(r, c):
        sl = pl.ds(pl.multiple_of(r * rows, rows), rows)
        dst_ref[sl, :] = src_ref[sl, :].astype(dst_ref.dtype)
        return c

    lax.fori_loop(0, n // rows, body, 0)


def _rms(x, g):
    ms = jnp.mean(x * x, axis=-1, keepdims=True)
    return x * lax.rsqrt(ms + EPS) * g


def _rmsnorm_kernel(x_ref, g_ref, o_ref):
    o_ref[...] = _rms(x_ref[...], g_ref[...]).astype(o_ref.dtype)


def _rmsnorm(x, g, out_dtype, tm=1024):
    m, d = x.shape
    tm = min(tm, m)
    return pl.pallas_call(
        _rmsnorm_kernel,
        grid=(m // tm,),
        in_specs=[pl.BlockSpec((tm, d), lambda i: (i, 0)),
                  pl.BlockSpec((1, d), lambda i: (0, 0))],
        out_specs=pl.BlockSpec((tm, d), lambda i: (i, 0)),
        out_shape=jax.ShapeDtypeStruct((m, d), out_dtype),
        compiler_params=_params(1), name="rmsnorm",
    )(x, g.reshape(1, d))


def _rope_heads(acc, cos, sin):
    outs = []
    for h in range(acc.shape[1] // LANES):
        a = acc[:, h * LANES:(h + 1) * LANES]
        outs.append(a * cos + pltpu.roll(a, LANES // 2, axis=1) * sin)
    return outs[0] if len(outs) == 1 else jnp.concatenate(outs, axis=1)


def _mm_kernel(*refs, mode, scale, scale_cols, tn):
    if mode == "rope":
        x_ref, w_ref, cos_ref, sin_ref, o_ref, wb_ref = refs
    elif mode == "residual":
        x_ref, w_ref, res_ref, o_ref, wb_ref = refs
    else:
        x_ref, w_ref, o_ref, wb_ref = refs

    @pl.when(pl.program_id(1) == 0)
    def _():
        _cast_rows(wb_ref, w_ref, 256)

    acc = jnp.dot(x_ref[...], wb_ref[...], preferred_element_type=F32)
    if mode == "rope":
        acc = _rope_heads(acc, cos_ref[...], sin_ref[...]) * scale
    elif mode == "residual":
        acc = acc + res_ref[...]
    elif mode == "scale":
        acc = acc * scale
    elif mode == "scale_cols":
        s = jnp.where(pl.program_id(0) * tn < scale_cols, scale, 1.0).astype(F32)
        acc = acc * s
    o_ref[...] = acc.astype(o_ref.dtype)


def _mm(x, w, layer, *, tm, tn, out_dtype, mode="plain", scale=1.0, scale_cols=0,
        cos=None, sin=None, res=None):
    m, k = x.shape
    n = w.shape[2]
    tm, tn = min(tm, m), min(tn, n)
    in_specs = [pl.BlockSpec((tm, k), lambda j, i: (i, 0)),
                pl.BlockSpec((None, k, tn), lambda j, i: (layer, 0, j))]
    args = [x, w]
    if mode == "rope":
        in_specs += [pl.BlockSpec((tm, LANES), lambda j, i: (i, 0))] * 2
        args += [cos, sin]
    elif mode == "residual":
        in_specs += [pl.BlockSpec((tm, tn), lambda j, i: (i, j))]
        args += [res]
    return pl.pallas_call(
        functools.partial(_mm_kernel, mode=mode, scale=scale, scale_cols=scale_cols, tn=tn),
        grid=(n // tn, m // tm),
        in_specs=in_specs,
        out_specs=pl.BlockSpec((tm, tn), lambda j, i: (i, j)),
        out_shape=jax.ShapeDtypeStruct((m, n), out_dtype),
        scratch_shapes=[pltpu.VMEM((k, tn), BF16)],
        compiler_params=_params(2), name="mm_" + mode,
    )(*args)


def _mla_down_kernel(h_ref, w_ref, gq_ref, gkv_ref, cos_ref, sin_ref,
                     cq_ref, ckv_ref, kr_ref, wb_ref):
    @pl.when(pl.program_id(0) == 0)
    def _():
        _cast_rows(wb_ref, w_ref, 256)

    acc = jnp.dot(h_ref[...], wb_ref[...], preferred_element_type=F32)
    cq_ref[...] = _rms(acc[:, :Q_LORA], gq_ref[...]).astype(BF16)
    ckv_ref[...] = _rms(acc[:, Q_LORA:Q_LORA + KV_LORA], gkv_ref[...]).astype(BF16)
    kr = acc[:, Q_LORA + KV_LORA:]
    kr_ref[...] = _rope_heads(kr, cos_ref[...], sin_ref[...]).astype(BF16)


def _mla_down(h, w, gq, gkv, cos, sin, tm=1024):
    m, k = h.shape
    n = w.shape[1]
    tm = min(tm, m)
    row = lambda i: (i, 0)
    fixed = lambda i: (0, 0)
    return pl.pallas_call(
        _mla_down_kernel,
        grid=(m // tm,),
        in_specs=[pl.BlockSpec((tm, k), row),
                  pl.BlockSpec((k, n), fixed),
                  pl.BlockSpec((1, Q_LORA), fixed),
                  pl.BlockSpec((1, KV_LORA), fixed),
                  pl.BlockSpec((tm, LANES), row),
                  pl.BlockSpec((tm, LANES), row)],
        out_specs=[pl.BlockSpec((tm, Q_LORA), row),
                   pl.BlockSpec((tm, KV_LORA), row),
                   pl.BlockSpec((tm, LANES), row)],
        out_shape=[jax.ShapeDtypeStruct((m, Q_LORA), BF16),
                   jax.ShapeDtypeStruct((m, KV_LORA), BF16),
                   jax.ShapeDtypeStruct((m, LANES), BF16)],
        scratch_shapes=[pltpu.VMEM((k, n), BF16)],
        compiler_params=_params(1), name="mla_down",
    )(h, w, gq.reshape(1, -1), gkv.reshape(1, -1), cos, sin)


def _mla_attn_kernel(qn_ref, qr_ref, kv_ref, kr_ref, o_ref,
                     s_ref, p_ref, m_ref, l_ref, acc_ref, *, tq, tk, nh, rg):
    i = pl.program_id(1)
    hd = NOPE_DIM
    m_ref[...] = jnp.full(m_ref.shape, NEG_BIG, F32)
    l_ref[...] = jnp.zeros_like(l_ref)
    acc_ref[...] = jnp.zeros_like(acc_ref)

    def step(kb, diag):
        rows = pl.ds(pl.multiple_of(kb * tk, tk), tk)
        kr = kr_ref[rows, :]
        for h in range(nh):
            q = jnp.concatenate([qn_ref[:, h * hd:(h + 1) * hd],
                                 qr_ref[:, h * hd:(h + 1) * hd]], axis=1)
            k = jnp.concatenate([kv_ref[rows, 2 * h * hd:(2 * h + 1) * hd], kr], axis=1)
            s = lax.dot_general(q, k, (((1,), (1,)), ((), ())),
                                preferred_element_type=F32)
            if diag is not None:
                qc = lax.broadcasted_iota(jnp.int32, (tq, tk), 0) // CHUNK
                kc = (lax.broadcasted_iota(jnp.int32, (tq, tk), 1) + diag * tk) // CHUNK
                s = jnp.where(kc <= qc, s, NEG_BIG)
            s_ref[h] = s
        nt = tk // LANES
        alphas = []
        for h in range(nh):
            m_tile = s_ref[h, :, 0:LANES]
            for c in range(1, nt):
                m_tile = jnp.maximum(m_tile, s_ref[h, :, c * LANES:(c + 1) * LANES])
            m_prev = m_ref[h]
            m_rep = jnp.maximum(m_prev, jnp.max(m_tile, axis=1, keepdims=True))
            m_ref[h] = m_rep
            alpha = jnp.exp2(m_prev - m_rep)
            alphas.append(alpha)
            for r in range(tq // rg):
                rs = slice(r * rg, (r + 1) * rg)
                part = None
                for c in range(nt):
                    cs = slice(c * LANES, (c + 1) * LANES)
                    p = jnp.exp2(s_ref[h, rs, cs] - m_rep[rs])
                    p_ref[h, rs, cs] = p.astype(BF16)
                    part = p if part is None else part + p
                l_ref[h, rs, :] = alpha[rs] * l_ref[h, rs, :] + part
        for h in range(nh):
            v = kv_ref[rows, (2 * h + 1) * hd:(2 * h + 2) * hd]
            acc_ref[h] = alphas[h] * acc_ref[h] + jnp.dot(p_ref[h], v, preferred_element_type=F32)

    def body(kb, c):
        step(kb, None)
        return c

    nd = tq // tk
    lax.fori_loop(0, i * nd, body, 0)
    for d in range(nd):
        step(i * nd + d, d)
    for h in range(nh):
        l = jnp.sum(l_ref[h], axis=1, keepdims=True)
        o_ref[:, h * hd:(h + 1) * hd] = (acc_ref[h] / l).astype(o_ref.dtype)


def _mla_attn(qn, qr, kv, kr, tq=512, tk=512, nh=4, rg=32):
    s = qn.shape[0]
    tq, tk = min(tq, s), min(tk, s)
    assert CHUNK % rg == 0 and tk % CHUNK == 0 and tq % tk == 0
    return pl.pallas_call(
        functools.partial(_mla_attn_kernel, tq=tq, tk=tk, nh=nh, rg=rg),
        scratch_shapes=[pltpu.VMEM((nh, tq, tk), F32), pltpu.VMEM((nh, tq, tk), BF16),
                        pltpu.VMEM((nh, tq, LANES), F32), pltpu.VMEM((nh, tq, LANES), F32),
                        pltpu.VMEM((nh, tq, V_DIM), F32)],
        grid=(MLA_HEADS // nh, s // tq),
        in_specs=[pl.BlockSpec((tq, nh * NOPE_DIM), lambda g, i: (i, g)),
                  pl.BlockSpec((tq, nh * LANES), lambda g, i: (i, g)),
                  pl.BlockSpec((s, nh * (NOPE_DIM + V_DIM)), lambda g, i: (0, g)),
                  pl.BlockSpec((s, LANES), lambda g, i: (0, 0))],
        out_specs=pl.BlockSpec((tq, nh * V_DIM), lambda g, i: (i, g)),
        out_shape=jax.ShapeDtypeStruct((s, MLA_HEADS * V_DIM), BF16),
        compiler_params=_params(2), name="mla_attn",
    )(qn, qr, kv, kr)


def _sb_attn_kernel(q_ref, k_ref, v_ref, o_ref, carry_ref, acc_ref, *, tq, nh):
    i = pl.program_id(1)
    hd = SB_HEAD_DIM
    row = lax.broadcasted_iota(jnp.int32, (tq, tq), 0)
    col = lax.broadcasted_iota(jnp.int32, (tq, tq), 1)
    upper = jnp.where(row > col, 1.0, 0.0).astype(BF16)
    carry_ref[...] = jnp.zeros_like(carry_ref)
    acc_ref[...] = jnp.zeros_like(acc_ref)

    def step(kb, diagonal):
        rows = pl.ds(pl.multiple_of(kb * tq, tq), tq)
        heads = range(nh)
        hcols = [slice(h * hd, (h + 1) * hd) for h in heads]
        z = [lax.dot_general(q_ref[:, hcols[h]], k_ref[rows, hcols[h]], (((1,), (1,)), ((), ())),
                             preferred_element_type=F32) for h in heads]
        sp = [jnp.maximum(z[h], 0.0) + jnp.log2(1.0 + jnp.exp2(-jnp.abs(z[h]))) for h in heads]
        lk = [jnp.where(col < row, -sp[h], 0.0) if diagonal else -sp[h] for h in heads]
        later = []
        for h in heads:
            lk_hi = lk[h].astype(BF16)
            lk_lo = (lk[h] - lk_hi.astype(F32)).astype(BF16)
            later.append(jnp.dot(lk_hi, upper, preferred_element_type=F32)
                         + jnp.dot(lk_lo, upper, preferred_element_type=F32))
        top = None
        for h in heads:
            carry = carry_ref[h]
            a = jnp.exp2(z[h] - sp[h] + later[h] + jnp.tile(carry, (1, tq // LANES)))
            if diagonal:
                a = jnp.where(col < row, a, 0.0)
            acc_ref[:, hcols[h]] += jnp.dot(a.astype(BF16), v_ref[rows, hcols[h]],
                                            preferred_element_type=F32)
            carry = carry + jnp.sum(lk[h], axis=1, keepdims=True)
            carry_ref[h] = carry
            top = carry if top is None else jnp.maximum(top, carry)
        return jnp.max(top) > -SB_STOP * LOG2E

    def cond(state):
        kb, live = state
        return jnp.logical_and(kb >= 0, live)

    def body(state):
        kb, _ = state
        return kb - 1, step(kb, False)

    lax.while_loop(cond, body, (i - 1, step(i, True)))
    o_ref[...] = acc_ref[...].astype(o_ref.dtype)


def _sb_attn(qkv, tq=256, nh=4):
    s = qkv.shape[0]
    tq = min(tq, s)
    ng = SB_HEADS // nh
    w = nh * SB_HEAD_DIM
    return pl.pallas_call(
        functools.partial(_sb_attn_kernel, tq=tq, nh=nh),
        grid=(ng, s // tq),
        in_specs=[pl.BlockSpec((tq, w), lambda g, i: (i, g)),
                  pl.BlockSpec((s, w), lambda g, i: (0, ng + g)),
                  pl.BlockSpec((s, w), lambda g, i: (0, 2 * ng + g))],
        out_specs=pl.BlockSpec((tq, w), lambda g, i: (i, g)),
        out_shape=jax.ShapeDtypeStruct((s, SB_HEADS * SB_HEAD_DIM), BF16),
        scratch_shapes=[pltpu.VMEM((nh, tq, LANES), F32), pltpu.VMEM((tq, w), F32)],
        compiler_params=_params(2), name="sb_attn",
    )(qkv, qkv, qkv)


def _proj_res_norm_kernel(a_ref, w_ref, res_ref, g_ref, x_ref, h_ref, wb_ref):
    @pl.when(pl.program_id(0) == 0)
    def _():
        _cast_rows(wb_ref, w_ref, 256)

    x = res_ref[...] + jnp.dot(a_ref[...], wb_ref[...], preferred_element_type=F32)
    x_ref[...] = x
    h_ref[...] = _rms(x, g_ref[...]).astype(BF16)


def _proj_res_norm(a, w, layer, res, g, tm=512):
    m, k = a.shape
    n = w.shape[2]
    tm = min(tm, m)
    row = lambda i: (i, 0)
    fixed = lambda i: (0, 0)
    return pl.pallas_call(
        _proj_res_norm_kernel,
        grid=(m // tm,),
        in_specs=[pl.BlockSpec((tm, k), row),
                  pl.BlockSpec((None, k, n), lambda i: (layer, 0, 0),
                               pipeline_mode=pl.Buffered(1)),
                  pl.BlockSpec((tm, n), row),
                  pl.BlockSpec((1, n), fixed)],
        out_specs=[pl.BlockSpec((tm, n), row), pl.BlockSpec((tm, n), row)],
        out_shape=[jax.ShapeDtypeStruct((m, n), F32),
                   jax.ShapeDtypeStruct((m, n), BF16)],
        scratch_shapes=[pltpu.VMEM((k, n), BF16)],
        compiler_params=_params(1), name="proj_res_norm",
    )(a, w, res, g.reshape(1, n))


def _mem_kv_kernel(mem_ref, g_ref, w_ref, o_ref):
    hm = _rms(mem_ref[...], g_ref[0]).astype(BF16)
    half = w_ref.shape[2] // 2
    for c in range(2):
        w = w_ref[0, :, c * half:(c + 1) * half].astype(BF16)
        o_ref[0, :, c * half:(c + 1) * half] = jnp.dot(
            hm, w, preferred_element_type=F32).astype(BF16)


def _mem_kv(mem, g, w):
    depth, d, n = w.shape
    nm = mem.shape[0]
    return pl.pallas_call(
        _mem_kv_kernel,
        grid=(depth,),
        in_specs=[pl.BlockSpec((nm, d), lambda l: (0, 0)),
                  pl.BlockSpec((1, 1, d), lambda l: (l, 0, 0)),
                  pl.BlockSpec((1, d, n), lambda l: (l, 0, 0))],
        out_specs=pl.BlockSpec((1, nm, n), lambda l: (l, 0, 0)),
        out_shape=jax.ShapeDtypeStruct((depth, nm, n), BF16),
        compiler_params=_params(1), name="mem_kv",
    )(mem, g.reshape(depth, 1, d), w)


def _mem_attn_kernel(h_ref, x_ref, wq_ref, kv_ref, wo_ref, g_ref,
                     xo_ref, ho_ref, wqb_ref, wob_ref):
    @pl.when(pl.program_id(0) == 0)
    def _():
        _cast_rows(wqb_ref, wq_ref, 256)
        _cast_rows(wob_ref, wo_ref, 256)

    hd = MEM_HEAD_DIM
    nk = MEM_HEADS * hd
    nm = kv_ref.shape[1]
    heads = range(MEM_HEADS)
    scale = LOG2E / math.sqrt(hd)
    q = (jnp.dot(h_ref[...], wqb_ref[...], preferred_element_type=F32) * scale).astype(BF16)
    s = [lax.dot_general(q[:, h * hd:(h + 1) * hd], kv_ref[0, :, h * hd:(h + 1) * hd],
                         (((1,), (1,)), ((), ())), preferred_element_type=F32)
         for h in heads]
    p = [jnp.exp2(s[h] - jnp.max(s[h], axis=1, keepdims=True)).astype(BF16) for h in heads]
    ones = jnp.ones((nm, hd), BF16)
    o = []
    for h in heads:
        v1 = jnp.concatenate([kv_ref[0, :, nk + h * hd:nk + (h + 1) * hd], ones], axis=1)
        ol = jnp.dot(p[h], v1, preferred_element_type=F32)
        o.append((ol[:, :hd] / ol[:, hd:]).astype(BF16))
    x = x_ref[...] + jnp.dot(jnp.concatenate(o, axis=1), wob_ref[...], preferred_element_type=F32)
    xo_ref[...] = x
    ho_ref[...] = _rms(x, g_ref[...]).astype(BF16)


def _mem_attn(h, x, wq, kv_all, layer, wo, g, tm=512):
    m, d = h.shape
    nq = wq.shape[2]
    tm = min(tm, m)
    row = lambda i: (i, 0)
    fixed = lambda i: (0, 0)
    at_layer = lambda i: (layer, 0, 0)
    nm, nkv = kv_all.shape[1:]
    return pl.pallas_call(
        _mem_attn_kernel,
        grid=(m // tm,),
        in_specs=[pl.BlockSpec((tm, d), row),
                  pl.BlockSpec((tm, d), row),
                  pl.BlockSpec((None, d, nq), at_layer, pipeline_mode=pl.Buffered(1)),
                  pl.BlockSpec((1, nm, nkv), at_layer),
                  pl.BlockSpec((None, nq, d), at_layer, pipeline_mode=pl.Buffered(1)),
                  pl.BlockSpec((1, d), fixed)],
        out_specs=[pl.BlockSpec((tm, d), row), pl.BlockSpec((tm, d), row)],
        out_shape=[jax.ShapeDtypeStruct((m, d), F32),
                   jax.ShapeDtypeStruct((m, d), BF16)],
        scratch_shapes=[pltpu.VMEM((d, nq), BF16), pltpu.VMEM((nq, d), BF16)],
        compiler_params=_params(1), name="mem_attn",
    )(h, x, wq, kv_all, wo, g.reshape(1, d))


def _conv3(u_ref, r0, nr, cols, cw, cb):
    a = SUBLANES + r0
    return (cb + cw[2:3, :] * u_ref[a:a + nr, cols] + cw[1:2, :] * u_ref[a - 1:a - 1 + nr, cols]
            + cw[0:1, :] * u_ref[a - 2:a - 2 + nr, cols])


def _silu_gate(gate, up):
    return gate * (1.0 / (1.0 + jnp.exp(-gate))) * up


def _ffn_in_kernel(h_ref, wg_ref, wu_ref, cwg_ref, cwu_ref, cbg_ref, cbu_ref,
                   o_ref, wgb_ref, wub_ref, raw_ref, *, tn, tc, ni, nt):
    hs = SUBLANES
    tm = h_ref.shape[0]
    t = pl.program_id(0)
    slot = t % 2
    i_cur = jnp.minimum(t, nt - 1) % ni
    cur = (raw_ref.at[slot, 0], raw_ref.at[slot, 1])
    prev = (raw_ref.at[1 - slot, 0], raw_ref.at[1 - slot, 1])

    @pl.when(t == 0)
    def _():
        for r in prev:
            r[...] = jnp.zeros(r.shape, F32)

    @pl.when(jnp.logical_and(i_cur == 0, t < nt))
    def _():
        _cast_rows(wgb_ref, wg_ref, 256)
        _cast_rows(wub_ref, wu_ref, 256)
        for r in cur:
            r[:hs, :] = jnp.zeros((hs, tn), F32)

    @pl.when(i_cur > 0)
    def _():
        for r, p in zip(cur, prev):
            r[:hs, :] = p[tm:tm + hs, :]

    def epilogue(r0, nr):
        tok = jnp.zeros((hs, LANES), F32)
        for c in range(tn // tc):
            cols = slice(c * tc, (c + 1) * tc)
            gate = _conv3(prev[0], r0, nr, cols, cwg_ref[:, cols], cbg_ref[:, cols])
            up = _conv3(prev[1], r0, nr, cols, cwu_ref[:, cols], cbu_ref[:, cols])
            g = _silu_gate(gate, up)
            o_ref[r0:r0 + nr, cols] = g.astype(o_ref.dtype)
            part = jnp.sum(g.reshape(nr // hs, hs, tc), axis=0)
            for l in range(tc // LANES):
                tok = tok + part[:, l * LANES:(l + 1) * LANES]
        return tok

    h = h_ref[...]
    q = tm // 4
    tok = epilogue(0, q)
    acc = jnp.dot(h, wgb_ref[...], preferred_element_type=F32)
    cur[0][hs:, :] = acc
    zero = (pltpu.bitcast(tok, jnp.uint32) >> 16) >> 16
    cur[0][hs:2 * hs, :LANES] = jnp.where(zero == 0, acc[:hs, :LANES], acc[hs:2 * hs, :LANES])
    epilogue(q, tm - q)
    cur[1][hs:, :] = jnp.dot(h, wub_ref[...], preferred_element_type=F32)


def _ffn_in(h, w_in, conv_w, conv_b, layer, tm=1024, tn=512, tc=256):
    m, k = h.shape
    dff = w_in.shape[2] // 2
    tm = min(tm, m)
    nj, ni = dff // tn, m // tm
    nt = nj * ni
    conv_b = conv_b.reshape(conv_b.shape[0], 1, -1)
    mm_tile = lambda t: jnp.minimum(t, nt - 1)
    ep_tile = lambda t: jnp.maximum(t - 1, 0)
    gate_w = lambda t: (layer, 0, mm_tile(t) // ni)
    up_w = lambda t: (layer, 0, nj + mm_tile(t) // ni)
    gate_c = lambda t: (layer, 0, ep_tile(t) // ni)
    up_c = lambda t: (layer, 0, nj + ep_tile(t) // ni)
    return pl.pallas_call(
        functools.partial(_ffn_in_kernel, tn=tn, tc=tc, ni=ni, nt=nt),
        grid=(nt + 1,),
        in_specs=[pl.BlockSpec((tm, k), lambda t: (mm_tile(t) % ni, 0)),
                  pl.BlockSpec((None, k, tn), gate_w),
                  pl.BlockSpec((None, k, tn), up_w),
                  pl.BlockSpec((None, CONV_W, tn), gate_c),
                  pl.BlockSpec((None, CONV_W, tn), up_c),
                  pl.BlockSpec((None, 1, tn), gate_c),
                  pl.BlockSpec((None, 1, tn), up_c)],
        out_specs=pl.BlockSpec((tm, tn), lambda t: (ep_tile(t) % ni, ep_tile(t) // ni)),
        out_shape=jax.ShapeDtypeStruct((m, dff), BF16),
        scratch_shapes=[pltpu.VMEM((k, tn), BF16), pltpu.VMEM((k, tn), BF16),
                        pltpu.VMEM((2, 2, SUBLANES + tm, tn), F32)],
        compiler_params=_params(1), name="ffn_in",
    )(h, w_in, w_in, conv_w, conv_w, conv_b, conv_b)


def _rope_slots(t):
    half = ROPE_DIM // 2
    z = jnp.zeros(t.shape[:-1] + (LANES // 2 - half,), t.dtype)
    return jnp.concatenate([t[..., :half], z, t[..., half:], z], axis=-1)


def _rope_tables(positions):
    inv_freq = ROPE_THETA ** (-jnp.arange(0, ROPE_DIM, 2, dtype=F32) / ROPE_DIM)
    ang = positions.astype(F32)[:, None] * inv_freq
    cos, sin = jnp.cos(ang), jnp.sin(ang)
    return (_rope_slots(jnp.concatenate([cos, cos], axis=-1)),
            _rope_slots(jnp.concatenate([-sin, sin], axis=-1)))


def kernel(x, mem, positions, norm_mix, norm_mem_q, norm_mem_kv, norm_ffn, norm_final, mla_w_down, mla_q_norm, mla_w_uq, mla_kv_norm, mla_w_ukv, mla_w_o, sb_w_qkv, sb_w_o, mem_w_q, mem_w_kv, mem_w_o, ffn_w_in, ffn_conv_w, ffn_conv_b, ffn_w_out):
    assert x.shape[0] == 1, "batch is fixed to 1"
    xs = x[0]
    cos, sin = _rope_tables(positions[0])
    mem_kv = _mem_kv(mem[0], norm_mem_kv, mem_w_kv)
    mla_scale = LOG2E / math.sqrt(NOPE_DIM + ROPE_DIM)
    sb_scale = LOG2E / math.sqrt(SB_HEAD_DIM)

    h = _rmsnorm(xs, norm_mix[0], BF16)
    for i in range(DEPTH):
        j = i // N_MIXERS
        if i % N_MIXERS == 0:
            nlat = Q_LORA + KV_LORA
            w_down = jnp.concatenate(
                [mla_w_down[j][:, :nlat], _rope_slots(mla_w_down[j][:, nlat:])], axis=1)
            c_q, c_kv, k_rope = _mla_down(h, w_down, mla_q_norm[j], mla_kv_norm[j], cos, sin)
            w_uq = mla_w_uq[j].reshape(Q_LORA, MLA_HEADS, NOPE_DIM + ROPE_DIM)
            w_qn = w_uq[:, :, :NOPE_DIM].reshape(Q_LORA, MLA_HEADS * NOPE_DIM)
            w_qr = _rope_slots(w_uq[:, :, NOPE_DIM:]).reshape(Q_LORA, MLA_HEADS * LANES)
            qn = _mm(c_q, w_qn[None], 0, tm=2048, tn=1024, out_dtype=BF16, mode="scale",
                     scale=mla_scale)
            qr = _mm(c_q, w_qr[None], 0, tm=2048, tn=1024, out_dtype=BF16, mode="rope",
                     scale=mla_scale, cos=cos, sin=sin)
            kv = _mm(c_kv, mla_w_ukv, j, tm=2048, tn=1024, out_dtype=BF16)
            o = _mla_attn(qn, qr, kv, k_rope)
            w_o = mla_w_o
        else:
            qkv = _mm(h, sb_w_qkv, j, tm=1024, tn=1024, out_dtype=BF16, mode="scale_cols",
                      scale=sb_scale, scale_cols=SB_HEADS * SB_HEAD_DIM)
            o = _sb_attn(qkv)
            w_o = sb_w_o
        xs, hq = _proj_res_norm(o, w_o, j, xs, norm_mem_q[i])
        xs, hf = _mem_attn(hq, xs, mem_w_q, mem_kv, i, mem_w_o, norm_ffn[i])
        g = _ffn_in(hf, ffn_w_in, ffn_conv_w, ffn_conv_b, i)
        xs = _mm(g, ffn_w_out, i, tm=512, tn=512, out_dtype=F32, mode="residual", res=xs)
        last = i == DEPTH - 1
        h = _rmsnorm(xs, norm_final if last else norm_mix[i + 1], F32 if last else BF16)
    return h[None]
```

```python
import functools
import math

import jax
import jax.numpy as jnp
from jax import lax
from jax.experimental import pallas as pl
from jax.experimental.pallas import tpu as pltpu

F32 = jnp.float32
BF16 = jnp.bfloat16

D_MODEL = 2048
DEPTH = 4
CHUNK = 64
N_MIXERS = 2
MLA_HEADS = 16
Q_LORA = 512
KV_LORA = 512
NOPE_DIM = 128
ROPE_DIM = 64
V_DIM = 128
ROPE_THETA = 10000.0
SB_HEADS = 16
SB_HEAD_DIM = 128
N_MEM = 256
MEM_HEADS = 4
MEM_HEAD_DIM = 128
D_FF = 5632
CONV_W = 3
EPS = 1e-6

LANES = 128
SUBLANES = 8
VMEM_LIMIT = 56 * 1024 * 1024
NEG_BIG = -1e30
LOG2E = 1.4426950408889634
SB_STOP = 120.0


def _params(n_grid, flags=None):
    return pltpu.CompilerParams(
        dimension_semantics=("arbitrary",) * n_grid,
        vmem_limit_bytes=VMEM_LIMIT, flags=flags)


def _cast_rows(dst_ref, src_ref, rows):
    n = src_ref.shape[0]
    rows = min(rows, n)

    def body(r, c):
        sl = pl.ds(pl.multiple_of(r * rows, rows), rows)
        dst_ref[sl, :] = src_ref[sl, :].astype(dst_ref.dtype)
        return c

    lax.fori_loop(0, n // rows, body, 0)


def _rms(x, g):
    ms = jnp.mean(x * x, axis=-1, keepdims=True)
    return x * lax.rsqrt(ms + EPS) * g


def _rmsnorm_kernel(x_ref, g_ref, o_ref):
    o_ref[...] = _rms(x_ref[...], g_ref[...]).astype(o_ref.dtype)


def _rmsnorm(x, g, out_dtype, tm=1024):
    m, d = x.shape
    tm = min(tm, m)
    return pl.pallas_call(
        _rmsnorm_kernel,
        grid=(m // tm,),
        in_specs=[pl.BlockSpec((tm, d), lambda i: (i, 0)),
                  pl.BlockSpec((1, d), lambda i: (0, 0))],
        out_specs=pl.BlockSpec((tm, d), lambda i: (i, 0)),
        out_shape=jax.ShapeDtypeStruct((m, d), out_dtype),
        compiler_params=_params(1), name="rmsnorm",
    )(x, g.reshape(1, d))


def _rope_heads(acc, cos, sin):
    outs = []
    for h in range(acc.shape[1] // LANES):
        a = acc[:, h * LANES:(h + 1) * LANES]
        outs.append(a * cos + pltpu.roll(a, LANES // 2, axis=1) * sin)
    return outs[0] if len(outs) == 1 else jnp.concatenate(outs, axis=1)


def _mm_kernel(*refs, mode, scale, scale_cols, tn):
    if mode == "rope":
        x_ref, w_ref, cos_ref, sin_ref, o_ref, wb_ref = refs
    elif mode == "residual":
        x_ref, w_ref, res_ref, o_ref, wb_ref = refs
    else:
        x_ref, w_ref, o_ref, wb_ref = refs

    @pl.when(pl.program_id(1) == 0)
    def _():
        _cast_rows(wb_ref, w_ref, 256)

    acc = jnp.dot(x_ref[...], wb_ref[...], preferred_element_type=F32)
    if mode == "rope":
        acc = _rope_heads(acc, cos_ref[...], sin_ref[...]) * scale
    elif mode == "residual":
        acc = acc + res_ref[...]
    elif mode == "scale":
        acc = acc * scale
    elif mode == "scale_cols":
        s = jnp.where(pl.program_id(0) * tn < scale_cols, scale, 1.0).astype(F32)
        acc = acc * s
    o_ref[...] = acc.astype(o_ref.dtype)


def _mm(x, w, layer, *, tm, tn, out_dtype, mode="plain", scale=1.0, scale_cols=0,
        cos=None, sin=None, res=None):
    m, k = x.shape
    n = w.shape[2]
    tm, tn = min(tm, m), min(tn, n)
    in_specs = [pl.BlockSpec((tm, k), lambda j, i: (i, 0)),
                pl.BlockSpec((None, k, tn), lambda j, i: (layer, 0, j))]
    args = [x, w]
    if mode == "rope":
        in_specs += [pl.BlockSpec((tm, LANES), lambda j, i: (i, 0))] * 2
        args += [cos, sin]
    elif mode == "residual":
        in_specs += [pl.BlockSpec((tm, tn), lambda j, i: (i, j))]
        args += [res]
    return pl.pallas_call(
        functools.partial(_mm_kernel, mode=mode, scale=scale, scale_cols=scale_cols, tn=tn),
        grid=(n // tn, m // tm),
        in_specs=in_specs,
        out_specs=pl.BlockSpec((tm, tn), lambda j, i: (i, j)),
        out_shape=jax.ShapeDtypeStruct((m, n), out_dtype),
        scratch_shapes=[pltpu.VMEM((k, tn), BF16)],
        compiler_params=_params(2), name="mm_" + mode,
    )(*args)


def _mla_down_kernel(h_ref, w_ref, gq_ref, gkv_ref, cos_ref, sin_ref,
                     cq_ref, ckv_ref, kr_ref, wb_ref):
    @pl.when(pl.program_id(0) == 0)
    def _():
        _cast_rows(wb_ref, w_ref, 256)

    acc = jnp.dot(h_ref[...], wb_ref[...], preferred_element_type=F32)
    cq_ref[...] = _rms(acc[:, :Q_LORA], gq_ref[...]).astype(BF16)
    ckv_ref[...] = _rms(acc[:, Q_LORA:Q_LORA + KV_LORA], gkv_ref[...]).astype(BF16)
    kr = acc[:, Q_LORA + KV_LORA:]
    kr_ref[...] = _rope_heads(kr, cos_ref[...], sin_ref[...]).astype(BF16)


def _mla_down(h, w, gq, gkv, cos, sin, tm=1024):
    m, k = h.shape
    n = w.shape[1]
    tm = min(tm, m)
    row = lambda i: (i, 0)
    fixed = lambda i: (0, 0)
    return pl.pallas_call(
        _mla_down_kernel,
        grid=(m // tm,),
        in_specs=[pl.BlockSpec((tm, k), row),
                  pl.BlockSpec((k, n), fixed),
                  pl.BlockSpec((1, Q_LORA), fixed),
                  pl.BlockSpec((1, KV_LORA), fixed),
                  pl.BlockSpec((tm, LANES), row),
                  pl.BlockSpec((tm, LANES), row)],
        out_specs=[pl.BlockSpec((tm, Q_LORA), row),
                   pl.BlockSpec((tm, KV_LORA), row),
                   pl.BlockSpec((tm, LANES), row)],
        out_shape=[jax.ShapeDtypeStruct((m, Q_LORA), BF16),
                   jax.ShapeDtypeStruct((m, KV_LORA), BF16),
                   jax.ShapeDtypeStruct((m, LANES), BF16)],
        scratch_shapes=[pltpu.VMEM((k, n), BF16)],
        compiler_params=_params(1), name="mla_down",
    )(h, w, gq.reshape(1, -1), gkv.reshape(1, -1), cos, sin)


def _mla_attn_kernel(qn_ref, qr_ref, kv_ref, kr_ref, o_ref,
                     s_ref, p_ref, m_ref, l_ref, acc_ref, *, tq, tk, nh, rg):
    i = pl.program_id(1)
    hd = NOPE_DIM
    m_ref[...] = jnp.full(m_ref.shape, NEG_BIG, F32)
    l_ref[...] = jnp.zeros_like(l_ref)
    acc_ref[...] = jnp.zeros_like(acc_ref)

    def step(kb, diag):
        rows = pl.ds(pl.multiple_of(kb * tk, tk), tk)
        kr = kr_ref[rows, :]
        for h in range(nh):
            q = jnp.concatenate([qn_ref[:, h * hd:(h + 1) * hd],
                                 qr_ref[:, h * hd:(h + 1) * hd]], axis=1)
            k = jnp.concatenate([kv_ref[rows, 2 * h * hd:(2 * h + 1) * hd], kr], axis=1)
            s = lax.dot_general(q, k, (((1,), (1,)), ((), ())),
                                preferred_element_type=F32)
            if diag is not None:
                qc = lax.broadcasted_iota(jnp.int32, (tq, tk), 0) // CHUNK
                kc = (lax.broadcasted_iota(jnp.int32, (tq, tk), 1) + diag * tk) // CHUNK
                s = jnp.where(kc <= qc, s, NEG_BIG)
            s_ref[h] = s
        nt = tk // LANES
        alphas = []
        for h in range(nh):
            m_tile = s_ref[h, :, 0:LANES]
            for c in range(1, nt):
                m_tile = jnp.maximum(m_tile, s_ref[h, :, c * LANES:(c + 1) * LANES])
            m_prev = m_ref[h]
            m_rep = jnp.maximum(m_prev, jnp.max(m_tile, axis=1, keepdims=True))
            m_ref[h] = m_rep
            alpha = jnp.exp2(m_prev - m_rep)
            alphas.append(alpha)
            for r in range(tq // rg):
                rs = slice(r * rg, (r + 1) * rg)
                part = None
                for c in range(nt):
                    cs = slice(c * LANES, (c + 1) * LANES)
                    p = jnp.exp2(s_ref[h, rs, cs] - m_rep[rs])
                    p_ref[h, rs, cs] = p.astype(BF16)
                    part = p if part is None else part + p
                l_ref[h, rs, :] = alpha[rs] * l_ref[h, rs, :] + part
        for h in range(nh):
            v = kv_ref[rows, (2 * h + 1) * hd:(2 * h + 2) * hd]
            acc_ref[h] = alphas[h] * acc_ref[h] + jnp.dot(p_ref[h], v, preferred_element_type=F32)

    def body(kp, c):
        step(2 * kp, None)
        step(2 * kp + 1, None)
        return c

    nd = tq // tk
    n_full = i * nd
    lax.fori_loop(0, n_full // 2, body, 0)

    @pl.when(n_full % 2 == 1)
    def _():
        step(n_full - 1, None)

    for d in range(nd):
        step(n_full + d, d)
    for h in range(nh):
        l = jnp.sum(l_ref[h], axis=1, keepdims=True)
        o_ref[:, h * hd:(h + 1) * hd] = (acc_ref[h] / l).astype(o_ref.dtype)


def _mla_attn(qn, qr, kv, kr, tq=512, tk=512, nh=4, rg=32):
    s = qn.shape[0]
    tq, tk = min(tq, s), min(tk, s)
    assert CHUNK % rg == 0 and tk % CHUNK == 0 and tq % tk == 0
    return pl.pallas_call(
        functools.partial(_mla_attn_kernel, tq=tq, tk=tk, nh=nh, rg=rg),
        scratch_shapes=[pltpu.VMEM((nh, tq, tk), F32), pltpu.VMEM((nh, tq, tk), BF16),
                        pltpu.VMEM((nh, tq, LANES), F32), pltpu.VMEM((nh, tq, LANES), F32),
                        pltpu.VMEM((nh, tq, V_DIM), F32)],
        grid=(MLA_HEADS // nh, s // tq),
        in_specs=[pl.BlockSpec((tq, nh * NOPE_DIM), lambda g, i: (i, g)),
                  pl.BlockSpec((tq, nh * LANES), lambda g, i: (i, g)),
                  pl.BlockSpec((s, nh * (NOPE_DIM + V_DIM)), lambda g, i: (0, g)),
                  pl.BlockSpec((s, LANES), lambda g, i: (0, 0))],
        out_specs=pl.BlockSpec((tq, nh * V_DIM), lambda g, i: (i, g)),
        out_shape=jax.ShapeDtypeStruct((s, MLA_HEADS * V_DIM), BF16),
        compiler_params=_params(2), name="mla_attn",
    )(qn, qr, kv, kr)


def _sb_attn_kernel(q_ref, k_ref, v_ref, o_ref, carry_ref, acc_ref, *, tq, nh):
    i = pl.program_id(1)
    hd = SB_HEAD_DIM
    row = lax.broadcasted_iota(jnp.int32, (tq, tq), 0)
    col = lax.broadcasted_iota(jnp.int32, (tq, tq), 1)
    upper = jnp.where(row > col, 1.0, 0.0).astype(BF16)
    carry_ref[...] = jnp.zeros_like(carry_ref)
    acc_ref[...] = jnp.zeros_like(acc_ref)

    def step(kb, diagonal):
        rows = pl.ds(pl.multiple_of(kb * tq, tq), tq)
        heads = range(nh)
        hcols = [slice(h * hd, (h + 1) * hd) for h in heads]
        z = [lax.dot_general(q_ref[:, hcols[h]], k_ref[rows, hcols[h]], (((1,), (1,)), ((), ())),
                             preferred_element_type=F32) for h in heads]
        sp = [jnp.maximum(z[h], 0.0) + jnp.log2(1.0 + jnp.exp2(-jnp.abs(z[h]))) for h in heads]
        lk = [jnp.where(col < row, -sp[h], 0.0) if diagonal else -sp[h] for h in heads]
        later = []
        for h in heads:
            lk_hi = lk[h].astype(BF16)
            lk_lo = (lk[h] - lk_hi.astype(F32)).astype(BF16)
            later.append(jnp.dot(lk_hi, upper, preferred_element_type=F32)
                         + jnp.dot(lk_lo, upper, preferred_element_type=F32))
        top = None
        for h in heads:
            carry = carry_ref[h]
            a = jnp.exp2(z[h] - sp[h] + later[h] + jnp.tile(carry, (1, tq // LANES)))
            if diagonal:
                a = jnp.where(col < row, a, 0.0)
            acc_ref[:, hcols[h]] += jnp.dot(a.astype(BF16), v_ref[rows, hcols[h]],
                                            preferred_element_type=F32)
            carry = carry + jnp.sum(lk[h], axis=1, keepdims=True)
            carry_ref[h] = carry
            top = carry if top is None else jnp.maximum(top, carry)
        return jnp.max(top) > -SB_STOP * LOG2E

    def cond(state):
        kb, live = state
        return jnp.logical_and(kb >= 0, live)

    def body(state):
        kb, _ = state
        return kb - 1, step(kb, False)

    lax.while_loop(cond, body, (i - 1, step(i, True)))
    o_ref[...] = acc_ref[...].astype(o_ref.dtype)


def _sb_attn(qkv, tq=256, nh=4):
    s = qkv.shape[0]
    tq = min(tq, s)
    ng = SB_HEADS // nh
    w = nh * SB_HEAD_DIM
    return pl.pallas_call(
        functools.partial(_sb_attn_kernel, tq=tq, nh=nh),
        grid=(ng, s // tq),
        in_specs=[pl.BlockSpec((tq, w), lambda g, i: (i, g)),
                  pl.BlockSpec((s, w), lambda g, i: (0, ng + g)),
                  pl.BlockSpec((s, w), lambda g, i: (0, 2 * ng + g))],
        out_specs=pl.BlockSpec((tq, w), lambda g, i: (i, g)),
        out_shape=jax.ShapeDtypeStruct((s, SB_HEADS * SB_HEAD_DIM), BF16),
        scratch_shapes=[pltpu.VMEM((nh, tq, LANES), F32), pltpu.VMEM((tq, w), F32)],
        compiler_params=_params(2), name="sb_attn",
    )(qkv, qkv, qkv)


def _proj_res_norm_kernel(a_ref, w_ref, res_ref, g_ref, x_ref, h_ref, wb_ref):
    @pl.when(pl.program_id(0) == 0)
    def _():
        _cast_rows(wb_ref, w_ref, 256)

    x = res_ref[...] + jnp.dot(a_ref[...], wb_ref[...], preferred_element_type=F32)
    x_ref[...] = x
    h_ref[...] = _rms(x, g_ref[...]).astype(BF16)


def _proj_res_norm(a, w, layer, res, g, tm=512):
    m, k = a.shape
    n = w.shape[2]
    tm = min(tm, m)
    row = lambda i: (i, 0)
    fixed = lambda i: (0, 0)
    return pl.pallas_call(
        _proj_res_norm_kernel,
        grid=(m // tm,),
        in_specs=[pl.BlockSpec((tm, k), row),
                  pl.BlockSpec((None, k, n), lambda i: (layer, 0, 0),
                               pipeline_mode=pl.Buffered(1)),
                  pl.BlockSpec((tm, n), row),
                  pl.BlockSpec((1, n), fixed)],
        out_specs=[pl.BlockSpec((tm, n), row), pl.BlockSpec((tm, n), row)],
        out_shape=[jax.ShapeDtypeStruct((m, n), F32),
                   jax.ShapeDtypeStruct((m, n), BF16)],
        scratch_shapes=[pltpu.VMEM((k, n), BF16)],
        compiler_params=_params(1), name="proj_res_norm",
    )(a, w, res, g.reshape(1, n))


def _mem_kv_kernel(mem_ref, g_ref, w_ref, o_ref):
    hm = _rms(mem_ref[...], g_ref[0]).astype(BF16)
    half = w_ref.shape[2] // 2
    for c in range(2):
        w = w_ref[0, :, c * half:(c + 1) * half].astype(BF16)
        o_ref[0, :, c * half:(c + 1) * half] = jnp.dot(
            hm, w, preferred_element_type=F32).astype(BF16)


def _mem_kv(mem, g, w):
    depth, d, n = w.shape
    nm = mem.shape[0]
    return pl.pallas_call(
        _mem_kv_kernel,
        grid=(depth,),
        in_specs=[pl.BlockSpec((nm, d), lambda l: (0, 0)),
                  pl.BlockSpec((1, 1, d), lambda l: (l, 0, 0)),
                  pl.BlockSpec((1, d, n), lambda l: (l, 0, 0))],
        out_specs=pl.BlockSpec((1, nm, n), lambda l: (l, 0, 0)),
        out_shape=jax.ShapeDtypeStruct((depth, nm, n), BF16),
        compiler_params=_params(1), name="mem_kv",
    )(mem, g.reshape(depth, 1, d), w)


def _mem_attn_kernel(h_ref, x_ref, wq_ref, kv_ref, wo_ref, g_ref,
                     xo_ref, ho_ref, wqb_ref, wob_ref):
    @pl.when(pl.program_id(0) == 0)
    def _():
        _cast_rows(wqb_ref, wq_ref, 256)
        _cast_rows(wob_ref, wo_ref, 256)

    hd = MEM_HEAD_DIM
    nk = MEM_HEADS * hd
    nm = kv_ref.shape[1]
    heads = range(MEM_HEADS)
    scale = LOG2E / math.sqrt(hd)
    q = (jnp.dot(h_ref[...], wqb_ref[...], preferred_element_type=F32) * scale).astype(BF16)
    s = [lax.dot_general(q[:, h * hd:(h + 1) * hd], kv_ref[0, :, h * hd:(h + 1) * hd],
                         (((1,), (1,)), ((), ())), preferred_element_type=F32)
         for h in heads]
    p = [jnp.exp2(s[h] - jnp.max(s[h], axis=1, keepdims=True)).astype(BF16) for h in heads]
    ones = jnp.ones((nm, hd), BF16)
    o = []
    for h in heads:
        v1 = jnp.concatenate([kv_ref[0, :, nk + h * hd:nk + (h + 1) * hd], ones], axis=1)
        ol = jnp.dot(p[h], v1, preferred_element_type=F32)
        o.append((ol[:, :hd] / ol[:, hd:]).astype(BF16))
    x = x_ref[...] + jnp.dot(jnp.concatenate(o, axis=1), wob_ref[...], preferred_element_type=F32)
    xo_ref[...] = x
    ho_ref[...] = _rms(x, g_ref[...]).astype(BF16)


def _mem_attn(h, x, wq, kv_all, layer, wo, g, tm=512):
    m, d = h.shape
    nq = wq.shape[2]
    tm = min(tm, m)
    row = lambda i: (i, 0)
    fixed = lambda i: (0, 0)
    at_layer = lambda i: (layer, 0, 0)
    nm, nkv = kv_all.shape[1:]
    return pl.pallas_call(
        _mem_attn_kernel,
        grid=(m // tm,),
        in_specs=[pl.BlockSpec((tm, d), row),
                  pl.BlockSpec((tm, d), row),
                  pl.BlockSpec((None, d, nq), at_layer, pipeline_mode=pl.Buffered(1)),
                  pl.BlockSpec((1, nm, nkv), at_layer),
                  pl.BlockSpec((None, nq, d), at_layer, pipeline_mode=pl.Buffered(1)),
                  pl.BlockSpec((1, d), fixed)],
        out_specs=[pl.BlockSpec((tm, d), row), pl.BlockSpec((tm, d), row)],
        out_shape=[jax.ShapeDtypeStruct((m, d), F32),
                   jax.ShapeDtypeStruct((m, d), BF16)],
        scratch_shapes=[pltpu.VMEM((d, nq), BF16), pltpu.VMEM((nq, d), BF16)],
        compiler_params=_params(1), name="mem_attn",
    )(h, x, wq, kv_all, wo, g.reshape(1, d))


def _conv3(u_ref, r0, nr, cols, cw, cb):
    a = SUBLANES + r0
    return (cb + cw[2:3, :] * u_ref[a:a + nr, cols] + cw[1:2, :] * u_ref[a - 1:a - 1 + nr, cols]
            + cw[0:1, :] * u_ref[a - 2:a - 2 + nr, cols])


def _silu_gate(gate, up):
    return gate * (1.0 / (1.0 + jnp.exp(-gate))) * up


def _ffn_in_kernel(h_ref, wg_ref, wu_ref, cwg_ref, cwu_ref, cbg_ref, cbu_ref,
                   o_ref, wgb_ref, wub_ref, raw_ref, *, tn, tc, ni, nt):
    hs = SUBLANES
    tm = h_ref.shape[0]
    t = pl.program_id(0)
    slot = t % 2
    i_cur = jnp.minimum(t, nt - 1) % ni
    cur = (raw_ref.at[slot, 0], raw_ref.at[slot, 1])
    prev = (raw_ref.at[1 - slot, 0], raw_ref.at[1 - slot, 1])

    @pl.when(t == 0)
    def _():
        for r in prev:
            r[...] = jnp.zeros(r.shape, F32)

    @pl.when(jnp.logical_and(i_cur == 0, t < nt))
    def _():
        _cast_rows(wgb_ref, wg_ref, 256)
        _cast_rows(wub_ref, wu_ref, 256)
        for r in cur:
            r[:hs, :] = jnp.zeros((hs, tn), F32)

    @pl.when(i_cur > 0)
    def _():
        for r, p in zip(cur, prev):
            r[:hs, :] = p[tm:tm + hs, :]

    def epilogue(r0, nr):
        tok = jnp.zeros((hs, LANES), F32)
        for c in range(tn // tc):
            cols = slice(c * tc, (c + 1) * tc)
            gate = _conv3(prev[0], r0, nr, cols, cwg_ref[:, cols], cbg_ref[:, cols])
            up = _conv3(prev[1], r0, nr, cols, cwu_ref[:, cols], cbu_ref[:, cols])
            g = _silu_gate(gate, up)
            o_ref[r0:r0 + nr, cols] = g.astype(o_ref.dtype)
            part = jnp.sum(g.reshape(nr // hs, hs, tc), axis=0)
            for l in range(tc // LANES):
                tok = tok + part[:, l * LANES:(l + 1) * LANES]
        return tok

    h = h_ref[...]
    q = tm // 4
    tok = epilogue(0, q)
    acc = jnp.dot(h, wgb_ref[...], preferred_element_type=F32)
    cur[0][hs:, :] = acc
    zero = (pltpu.bitcast(tok, jnp.uint32) >> 16) >> 16
    cur[0][hs:2 * hs, :LANES] = jnp.where(zero == 0, acc[:hs, :LANES], acc[hs:2 * hs, :LANES])
    epilogue(q, tm - q)
    cur[1][hs:, :] = jnp.dot(h, wub_ref[...], preferred_element_type=F32)


def _ffn_in(h, w_in, conv_w, conv_b, layer, tm=1024, tn=512, tc=256):
    m, k = h.shape
    dff = w_in.shape[2] // 2
    tm = min(tm, m)
    nj, ni = dff // tn, m // tm
    nt = nj * ni
    conv_b = conv_b.reshape(conv_b.shape[0], 1, -1)
    mm_tile = lambda t: jnp.minimum(t, nt - 1)
    ep_tile = lambda t: jnp.maximum(t - 1, 0)
    gate_w = lambda t: (layer, 0, mm_tile(t) // ni)
    up_w = lambda t: (layer, 0, nj + mm_tile(t) // ni)
    gate_c = lambda t: (layer, 0, ep_tile(t) // ni)
    up_c = lambda t: (layer, 0, nj + ep_tile(t) // ni)
    return pl.pallas_call(
        functools.partial(_ffn_in_kernel, tn=tn, tc=tc, ni=ni, nt=nt),
        grid=(nt + 1,),
        in_specs=[pl.BlockSpec((tm, k), lambda t: (mm_tile(t) % ni, 0)),
                  pl.BlockSpec((None, k, tn), gate_w),
                  pl.BlockSpec((None, k, tn), up_w),
                  pl.BlockSpec((None, CONV_W, tn), gate_c),
                  pl.BlockSpec((None, CONV_W, tn), up_c),
                  pl.BlockSpec((None, 1, tn), gate_c),
                  pl.BlockSpec((None, 1, tn), up_c)],
        out_specs=pl.BlockSpec((tm, tn), lambda t: (ep_tile(t) % ni, ep_tile(t) // ni)),
        out_shape=jax.ShapeDtypeStruct((m, dff), BF16),
        scratch_shapes=[pltpu.VMEM((k, tn), BF16), pltpu.VMEM((k, tn), BF16),
                        pltpu.VMEM((2, 2, SUBLANES + tm, tn), F32)],
        compiler_params=_params(1), name="ffn_in",
    )(h, w_in, w_in, conv_w, conv_w, conv_b, conv_b)


def _rope_slots(t):
    half = ROPE_DIM // 2
    z = jnp.zeros(t.shape[:-1] + (LANES // 2 - half,), t.dtype)
    return jnp.concatenate([t[..., :half], z, t[..., half:], z], axis=-1)


def _rope_tables(positions):
    inv_freq = ROPE_THETA ** (-jnp.arange(0, ROPE_DIM, 2, dtype=F32) / ROPE_DIM)
    ang = positions.astype(F32)[:, None] * inv_freq
    cos, sin = jnp.cos(ang), jnp.sin(ang)
    return (_rope_slots(jnp.concatenate([cos, cos], axis=-1)),
            _rope_slots(jnp.concatenate([-sin, sin], axis=-1)))


def kernel(x, mem, positions, norm_mix, norm_mem_q, norm_mem_kv, norm_ffn, norm_final, mla_w_down, mla_q_norm, mla_w_uq, mla_kv_norm, mla_w_ukv, mla_w_o, sb_w_qkv, sb_w_o, mem_w_q, mem_w_kv, mem_w_o, ffn_w_in, ffn_conv_w, ffn_conv_b, ffn_w_out):
    assert x.shape[0] == 1, "batch is fixed to 1"
    xs = x[0]
    cos, sin = _rope_tables(positions[0])
    mem_kv = _mem_kv(mem[0], norm_mem_kv, mem_w_kv)
    mla_scale = LOG2E / math.sqrt(NOPE_DIM + ROPE_DIM)
    sb_scale = LOG2E / math.sqrt(SB_HEAD_DIM)

    h = _rmsnorm(xs, norm_mix[0], BF16)
    for i in range(DEPTH):
        j = i // N_MIXERS
        if i % N_MIXERS == 0:
            nlat = Q_LORA + KV_LORA
            w_down = jnp.concatenate(
                [mla_w_down[j][:, :nlat], _rope_slots(mla_w_down[j][:, nlat:])], axis=1)
            c_q, c_kv, k_rope = _mla_down(h, w_down, mla_q_norm[j], mla_kv_norm[j], cos, sin)
            w_uq = mla_w_uq[j].reshape(Q_LORA, MLA_HEADS, NOPE_DIM + ROPE_DIM)
            w_qn = w_uq[:, :, :NOPE_DIM].reshape(Q_LORA, MLA_HEADS * NOPE_DIM)
            w_qr = _rope_slots(w_uq[:, :, NOPE_DIM:]).reshape(Q_LORA, MLA_HEADS * LANES)
            qn = _mm(c_q, w_qn[None], 0, tm=2048, tn=1024, out_dtype=BF16, mode="scale",
                     scale=mla_scale)
            qr = _mm(c_q, w_qr[None], 0, tm=2048, tn=1024, out_dtype=BF16, mode="rope",
                     scale=mla_scale, cos=cos, sin=sin)
            kv = _mm(c_kv, mla_w_ukv, j, tm=2048, tn=1024, out_dtype=BF16)
            o = _mla_attn(qn, qr, kv, k_rope)
            w_o = mla_w_o
        else:
            qkv = _mm(h, sb_w_qkv, j, tm=1024, tn=1024, out_dtype=BF16, mode="scale_cols",
                      scale=sb_scale, scale_cols=SB_HEADS * SB_HEAD_DIM)
            o = _sb_attn(qkv)
            w_o = sb_w_o
        xs, hq = _proj_res_norm(o, w_o, j, xs, norm_mem_q[i])
        xs, hf = _mem_attn(hq, xs, mem_w_q, mem_kv, i, mem_w_o, norm_ffn[i])
        g = _ffn_in(hf, ffn_w_in, ffn_conv_w, ffn_conv_b, i)
        xs = _mm(g, ffn_w_out, i, tm=512, tn=512, out_dtype=F32, mode="residual", res=xs)
        last = i == DEPTH - 1
        h = _rmsnorm(xs, norm_final if last else norm_mix[i + 1], F32 if last else BF16)
    return h[None]
```

```python
import functools
import math

import jax
import jax.numpy as jnp
from jax import lax
from jax.experimental import pallas as pl
from jax.experimental.pallas import tpu as pltpu

F32 = jnp.float32
BF16 = jnp.bfloat16

D_MODEL = 2048
DEPTH = 4
CHUNK = 64
N_MIXERS = 2
MLA_HEADS = 16
Q_LORA = 512
KV_LORA = 512
NOPE_DIM = 128
ROPE_DIM = 64
V_DIM = 128
ROPE_THETA = 10000.0
SB_HEADS = 16
SB_HEAD_DIM = 128
N_MEM = 256
MEM_HEADS = 4
MEM_HEAD_DIM = 128
D_FF = 5632
CONV_W = 3
EPS = 1e-6

LANES = 128
SUBLANES = 8
VMEM_LIMIT = 56 * 1024 * 1024
NEG_BIG = -1e30
LOG2E = 1.4426950408889634
SB_STOP = 120.0


def _params(n_grid, flags=None):
    return pltpu.CompilerParams(
        dimension_semantics=("arbitrary",) * n_grid,
        vmem_limit_bytes=VMEM_LIMIT, flags=flags)


def _cast_rows(dst_ref, src_ref, rows):
    n = src_ref.shape[0]
    rows = min(rows, n)

    def body(r, c):
        sl = pl.ds(pl.multiple_of(r * rows, rows), rows)
        dst_ref[sl, :] = src_ref[sl, :].astype(dst_ref.dtype)
        return c

    lax.fori_loop(0, n // rows, body, 0)


def _rms(x, g):
    ms = jnp.mean(x * x, axis=-1, keepdims=True)
    return x * lax.rsqrt(ms + EPS) * g


def _rmsnorm_kernel(x_ref, g_ref, o_ref):
    o_ref[...] = _rms(x_ref[...], g_ref[...]).astype(o_ref.dtype)


def _rmsnorm(x, g, out_dtype, tm=1024):
    m, d = x.shape
    tm = min(tm, m)
    return pl.pallas_call(
        _rmsnorm_kernel,
        grid=(m // tm,),
        in_specs=[pl.BlockSpec((tm, d), lambda i: (i, 0)),
                  pl.BlockSpec((1, d), lambda i: (0, 0))],
        out_specs=pl.BlockSpec((tm, d), lambda i: (i, 0)),
        out_shape=jax.ShapeDtypeStruct((m, d), out_dtype),
        compiler_params=_params(1), name="rmsnorm",
    )(x, g.reshape(1, d))


def _rope_heads(acc, cos, sin):
    outs = []
    for h in range(acc.shape[1] // LANES):
        a = acc[:, h * LANES:(h + 1) * LANES]
        outs.append(a * cos + pltpu.roll(a, LANES // 2, axis=1) * sin)
    return outs[0] if len(outs) == 1 else jnp.concatenate(outs, axis=1)


def _mm_kernel(*refs, mode, scale, scale_cols, tn):
    if mode == "rope":
        x_ref, w_ref, cos_ref, sin_ref, o_ref, wb_ref = refs
    elif mode == "residual":
        x_ref, w_ref, res_ref, o_ref, wb_ref = refs
    else:
        x_ref, w_ref, o_ref, wb_ref = refs

    @pl.when(pl.program_id(1) == 0)
    def _():
        _cast_rows(wb_ref, w_ref, 256)

    acc = jnp.dot(x_ref[...], wb_ref[...], preferred_element_type=F32)
    if mode == "rope":
        acc = _rope_heads(acc, cos_ref[...], sin_ref[...]) * scale
    elif mode == "residual":
        acc = acc + res_ref[...]
    elif mode == "scale":
        acc = acc * scale
    elif mode == "scale_cols":
        s = jnp.where(pl.program_id(0) * tn < scale_cols, scale, 1.0).astype(F32)
        acc = acc * s
    o_ref[...] = acc.astype(o_ref.dtype)


def _mm(x, w, layer, *, tm, tn, out_dtype, mode="plain", scale=1.0, scale_cols=0,
        cos=None, sin=None, res=None):
    m, k = x.shape
    n = w.shape[2]
    tm, tn = min(tm, m), min(tn, n)
    in_specs = [pl.BlockSpec((tm, k), lambda j, i: (i, 0)),
                pl.BlockSpec((None, k, tn), lambda j, i: (layer, 0, j))]
    args = [x, w]
    if mode == "rope":
        in_specs += [pl.BlockSpec((tm, LANES), lambda j, i: (i, 0))] * 2
        args += [cos, sin]
    elif mode == "residual":
        in_specs += [pl.BlockSpec((tm, tn), lambda j, i: (i, j))]
        args += [res]
    return pl.pallas_call(
        functools.partial(_mm_kernel, mode=mode, scale=scale, scale_cols=scale_cols, tn=tn),
        grid=(n // tn, m // tm),
        in_specs=in_specs,
        out_specs=pl.BlockSpec((tm, tn), lambda j, i: (i, j)),
        out_shape=jax.ShapeDtypeStruct((m, n), out_dtype),
        scratch_shapes=[pltpu.VMEM((k, tn), BF16)],
        compiler_params=_params(2), name="mm_" + mode,
    )(*args)


def _mla_down_kernel(h_ref, w_ref, gq_ref, gkv_ref, cos_ref, sin_ref,
                     cq_ref, ckv_ref, kr_ref, wb_ref):
    @pl.when(pl.program_id(0) == 0)
    def _():
        _cast_rows(wb_ref, w_ref, 256)

    acc = jnp.dot(h_ref[...], wb_ref[...], preferred_element_type=F32)
    cq_ref[...] = _rms(acc[:, :Q_LORA], gq_ref[...]).astype(BF16)
    ckv_ref[...] = _rms(acc[:, Q_LORA:Q_LORA + KV_LORA], gkv_ref[...]).astype(BF16)
    kr = acc[:, Q_LORA + KV_LORA:]
    kr_ref[...] = _rope_heads(kr, cos_ref[...], sin_ref[...]).astype(BF16)


def _mla_down(h, w, gq, gkv, cos, sin, tm=1024):
    m, k = h.shape
    n = w.shape[1]
    tm = min(tm, m)
    row = lambda i: (i, 0)
    fixed = lambda i: (0, 0)
    return pl.pallas_call(
        _mla_down_kernel,
        grid=(m // tm,),
        in_specs=[pl.BlockSpec((tm, k), row),
                  pl.BlockSpec((k, n), fixed),
                  pl.BlockSpec((1, Q_LORA), fixed),
                  pl.BlockSpec((1, KV_LORA), fixed),
                  pl.BlockSpec((tm, LANES), row),
                  pl.BlockSpec((tm, LANES), row)],
        out_specs=[pl.BlockSpec((tm, Q_LORA), row),
                   pl.BlockSpec((tm, KV_LORA), row),
                   pl.BlockSpec((tm, LANES), row)],
        out_shape=[jax.ShapeDtypeStruct((m, Q_LORA), BF16),
                   jax.ShapeDtypeStruct((m, KV_LORA), BF16),
                   jax.ShapeDtypeStruct((m, LANES), BF16)],
        scratch_shapes=[pltpu.VMEM((k, n), BF16)],
        compiler_params=_params(1), name="mla_down",
    )(h, w, gq.reshape(1, -1), gkv.reshape(1, -1), cos, sin)


def _mla_attn_kernel(qn_ref, qr_ref, kv_ref, kr_ref, o_ref,
                     s_ref, p_ref, m_ref, l_ref, acc_ref, *, tq, tk, nh, rg):
    i = pl.program_id(1)
    hd = NOPE_DIM
    m_ref[...] = jnp.full(m_ref.shape, NEG_BIG, F32)
    l_ref[...] = jnp.zeros_like(l_ref)
    acc_ref[...] = jnp.zeros_like(acc_ref)

    def step(kb, diag):
        rows = pl.ds(pl.multiple_of(kb * tk, tk), tk)
        kr = kr_ref[rows, :]
        for h in range(nh):
            q = jnp.concatenate([qn_ref[:, h * hd:(h + 1) * hd],
                                 qr_ref[:, h * hd:(h + 1) * hd]], axis=1)
            k = jnp.concatenate([kv_ref[rows, 2 * h * hd:(2 * h + 1) * hd], kr], axis=1)
            s = lax.dot_general(q, k, (((1,), (1,)), ((), ())),
                                preferred_element_type=F32)
            if diag is not None:
                qc = lax.broadcasted_iota(jnp.int32, (tq, tk), 0) // CHUNK
                kc = (lax.broadcasted_iota(jnp.int32, (tq, tk), 1) + diag * tk) // CHUNK
                s = jnp.where(kc <= qc, s, NEG_BIG)
            s_ref[h] = s
        nt = tk // LANES
        alphas = []
        for h in range(nh):
            m_tile = s_ref[h, :, 0:LANES]
            for c in range(1, nt):
                m_tile = jnp.maximum(m_tile, s_ref[h, :, c * LANES:(c + 1) * LANES])
            m_prev = m_ref[h]
            m_rep = jnp.maximum(m_prev, jnp.max(m_tile, axis=1, keepdims=True))
            m_ref[h] = m_rep
            alpha = jnp.exp2(m_prev - m_rep)
            alphas.append(alpha)
            for r in range(tq // rg):
                rs = slice(r * rg, (r + 1) * rg)
                part = None
                for c in range(nt):
                    cs = slice(c * LANES, (c + 1) * LANES)
                    p = jnp.exp2(s_ref[h, rs, cs] - m_rep[rs])
                    p_ref[h, rs, cs] = p.astype(BF16)
                    part = p if part is None else part + p
                l_ref[h, rs, :] = alpha[rs] * l_ref[h, rs, :] + part
        for h in range(nh):
            v = kv_ref[rows, (2 * h + 1) * hd:(2 * h + 2) * hd]
            acc_ref[h] = alphas[h] * acc_ref[h] + jnp.dot(p_ref[h], v, preferred_element_type=F32)

    def body(kp, c):
        step(2 * kp, None)
        step(2 * kp + 1, None)
        return c

    nd = tq // tk
    n_full = i * nd
    lax.fori_loop(0, n_full // 2, body, 0)

    @pl.when(n_full % 2 == 1)
    def _():
        step(n_full - 1, None)

    for d in range(nd):
        step(n_full + d, d)
    for h in range(nh):
        l = jnp.sum(l_ref[h], axis=1, keepdims=True)
        o_ref[:, h * hd:(h + 1) * hd] = (acc_ref[h] / l).astype(o_ref.dtype)


def _mla_attn(qn, qr, kv, kr, tq=512, tk=512, nh=4, rg=32):
    s = qn.shape[0]
    tq, tk = min(tq, s), min(tk, s)
    assert CHUNK % rg == 0 and tk % CHUNK == 0 and tq % tk == 0
    return pl.pallas_call(
        functools.partial(_mla_attn_kernel, tq=tq, tk=tk, nh=nh, rg=rg),
        scratch_shapes=[pltpu.VMEM((nh, tq, tk), F32), pltpu.VMEM((nh, tq, tk), BF16),
                        pltpu.VMEM((nh, tq, LANES), F32), pltpu.VMEM((nh, tq, LANES), F32),
                        pltpu.VMEM((nh, tq, V_DIM), F32)],
        grid=(MLA_HEADS // nh, s // tq),
        in_specs=[pl.BlockSpec((tq, nh * NOPE_DIM), lambda g, i: (i, g)),
                  pl.BlockSpec((tq, nh * LANES), lambda g, i: (i, g)),
                  pl.BlockSpec((s, nh * (NOPE_DIM + V_DIM)), lambda g, i: (0, g)),
                  pl.BlockSpec((s, LANES), lambda g, i: (0, 0))],
        out_specs=pl.BlockSpec((tq, nh * V_DIM), lambda g, i: (i, g)),
        out_shape=jax.ShapeDtypeStruct((s, MLA_HEADS * V_DIM), BF16),
        compiler_params=_params(2), name="mla_attn",
    )(qn, qr, kv, kr)


def _sb_attn_kernel(q_ref, k_ref, v_ref, o_ref, carry_ref, acc_ref, live_ref, *, tq, nh):
    i = pl.program_id(1)
    hd = SB_HEAD_DIM
    row = lax.broadcasted_iota(jnp.int32, (tq, tq), 0)
    col = lax.broadcasted_iota(jnp.int32, (tq, tq), 1)
    upper = jnp.where(row > col, 1.0, 0.0).astype(BF16)
    carry_ref[...] = jnp.zeros_like(carry_ref)
    acc_ref[...] = jnp.zeros_like(acc_ref)

    def step(kb, diagonal):
        rows = pl.ds(pl.multiple_of(kb * tq, tq), tq)
        heads = range(nh)
        hcols = [slice(h * hd, (h + 1) * hd) for h in heads]
        z = [lax.dot_general(q_ref[:, hcols[h]], k_ref[rows, hcols[h]], (((1,), (1,)), ((), ())),
                             preferred_element_type=F32) for h in heads]
        sp = [jnp.maximum(z[h], 0.0) + jnp.log2(1.0 + jnp.exp2(-jnp.abs(z[h]))) for h in heads]
        lk = [jnp.where(col < row, -sp[h], 0.0) if diagonal else -sp[h] for h in heads]
        later = []
        for h in heads:
            lk_hi = lk[h].astype(BF16)
            lk_lo = (lk[h] - lk_hi.astype(F32)).astype(BF16)
            later.append(jnp.dot(lk_hi, upper, preferred_element_type=F32)
                         + jnp.dot(lk_lo, upper, preferred_element_type=F32))
        top = None
        for h in heads:
            carry = carry_ref[h]
            a = jnp.exp2(z[h] - sp[h] + later[h] + jnp.tile(carry, (1, tq // LANES)))
            if diagonal:
                a = jnp.where(col < row, a, 0.0)
            acc_ref[:, hcols[h]] += jnp.dot(a.astype(BF16), v_ref[rows, hcols[h]],
                                            preferred_element_type=F32)
            carry = carry + jnp.sum(lk[h], axis=1, keepdims=True)
            carry_ref[h] = carry
            top = carry if top is None else jnp.maximum(top, carry)
        return jnp.max(top) > -SB_STOP * LOG2E

    def cond(state):
        kb, live = state
        return jnp.logical_and(kb >= 0, live)

    def body(state):
        kb, _ = state
        return kb - 1, step(kb, False)

    @pl.when(i == 0)
    def _():
        step(0, True)
        live_ref[0] = 0

    @pl.when(i > 0)
    def _():
        step(i, True)
        live_ref[0] = step(i - 1, False).astype(jnp.int32)

    lax.while_loop(cond, body, (i - 2, live_ref[0] > 0))
    o_ref[...] = acc_ref[...].astype(o_ref.dtype)


def _sb_attn(qkv, tq=256, nh=4):
    s = qkv.shape[0]
    tq = min(tq, s)
    ng = SB_HEADS // nh
    w = nh * SB_HEAD_DIM
    return pl.pallas_call(
        functools.partial(_sb_attn_kernel, tq=tq, nh=nh),
        grid=(ng, s // tq),
        in_specs=[pl.BlockSpec((tq, w), lambda g, i: (i, g)),
                  pl.BlockSpec((s, w), lambda g, i: (0, ng + g)),
                  pl.BlockSpec((s, w), lambda g, i: (0, 2 * ng + g))],
        out_specs=pl.BlockSpec((tq, w), lambda g, i: (i, g)),
        out_shape=jax.ShapeDtypeStruct((s, SB_HEADS * SB_HEAD_DIM), BF16),
        scratch_shapes=[pltpu.VMEM((nh, tq, LANES), F32), pltpu.VMEM((tq, w), F32),
                        pltpu.SMEM((1,), jnp.int32)],
        compiler_params=_params(2), name="sb_attn",
    )(qkv, qkv, qkv)


def _proj_res_norm_kernel(a_ref, w_ref, res_ref, g_ref, x_ref, h_ref, wb_ref):
    @pl.when(pl.program_id(0) == 0)
    def _():
        _cast_rows(wb_ref, w_ref, 256)

    x = res_ref[...] + jnp.dot(a_ref[...], wb_ref[...], preferred_element_type=F32)
    x_ref[...] = x
    h_ref[...] = _rms(x, g_ref[...]).astype(BF16)


def _proj_res_norm(a, w, layer, res, g, tm=512):
    m, k = a.shape
    n = w.shape[2]
    tm = min(tm, m)
    row = lambda i: (i, 0)
    fixed = lambda i: (0, 0)
    return pl.pallas_call(
        _proj_res_norm_kernel,
        grid=(m // tm,),
        in_specs=[pl.BlockSpec((tm, k), row),
                  pl.BlockSpec((None, k, n), lambda i: (layer, 0, 0),
                               pipeline_mode=pl.Buffered(1)),
                  pl.BlockSpec((tm, n), row),
                  pl.BlockSpec((1, n), fixed)],
        out_specs=[pl.BlockSpec((tm, n), row), pl.BlockSpec((tm, n), row)],
        out_shape=[jax.ShapeDtypeStruct((m, n), F32),
                   jax.ShapeDtypeStruct((m, n), BF16)],
        scratch_shapes=[pltpu.VMEM((k, n), BF16)],
        compiler_params=_params(1), name="proj_res_norm",
    )(a, w, res, g.reshape(1, n))


def _mem_kv_kernel(mem_ref, g_ref, w_ref, o_ref):
    hm = _rms(mem_ref[...], g_ref[0]).astype(BF16)
    half = w_ref.shape[2] // 2
    for c in range(2):
        w = w_ref[0, :, c * half:(c + 1) * half].astype(BF16)
        o_ref[0, :, c * half:(c + 1) * half] = jnp.dot(
            hm, w, preferred_element_type=F32).astype(BF16)


def _mem_kv(mem, g, w):
    depth, d, n = w.shape
    nm = mem.shape[0]
    return pl.pallas_call(
        _mem_kv_kernel,
        grid=(depth,),
        in_specs=[pl.BlockSpec((nm, d), lambda l: (0, 0)),
                  pl.BlockSpec((1, 1, d), lambda l: (l, 0, 0)),
                  pl.BlockSpec((1, d, n), lambda l: (l, 0, 0))],
        out_specs=pl.BlockSpec((1, nm, n), lambda l: (l, 0, 0)),
        out_shape=jax.ShapeDtypeStruct((depth, nm, n), BF16),
        compiler_params=_params(1), name="mem_kv",
    )(mem, g.reshape(depth, 1, d), w)


def _mem_attn_kernel(h_ref, x_ref, wq_ref, kv_ref, wo_ref, g_ref,
                     xo_ref, ho_ref, wqb_ref, wob_ref):
    @pl.when(pl.program_id(0) == 0)
    def _():
        _cast_rows(wqb_ref, wq_ref, 256)
        _cast_rows(wob_ref, wo_ref, 256)

    hd = MEM_HEAD_DIM
    nk = MEM_HEADS * hd
    nm = kv_ref.shape[1]
    heads = range(MEM_HEADS)
    scale = LOG2E / math.sqrt(hd)
    q = (jnp.dot(h_ref[...], wqb_ref[...], preferred_element_type=F32) * scale).astype(BF16)
    s = [lax.dot_general(q[:, h * hd:(h + 1) * hd], kv_ref[0, :, h * hd:(h + 1) * hd],
                         (((1,), (1,)), ((), ())), preferred_element_type=F32)
         for h in heads]
    p = [jnp.exp2(s[h] - jnp.max(s[h], axis=1, keepdims=True)).astype(BF16) for h in heads]
    ones = jnp.ones((nm, hd), BF16)
    o = []
    for h in heads:
        v1 = jnp.concatenate([kv_ref[0, :, nk + h * hd:nk + (h + 1) * hd], ones], axis=1)
        ol = jnp.dot(p[h], v1, preferred_element_type=F32)
        o.append((ol[:, :hd] / ol[:, hd:]).astype(BF16))
    x = x_ref[...] + jnp.dot(jnp.concatenate(o, axis=1), wob_ref[...], preferred_element_type=F32)
    xo_ref[...] = x
    ho_ref[...] = _rms(x, g_ref[...]).astype(BF16)


def _mem_attn(h, x, wq, kv_all, layer, wo, g, tm=512):
    m, d = h.shape
    nq = wq.shape[2]
    tm = min(tm, m)
    row = lambda i: (i, 0)
    fixed = lambda i: (0, 0)
    at_layer = lambda i: (layer, 0, 0)
    nm, nkv = kv_all.shape[1:]
    return pl.pallas_call(
        _mem_attn_kernel,
        grid=(m // tm,),
        in_specs=[pl.BlockSpec((tm, d), row),
                  pl.BlockSpec((tm, d), row),
                  pl.BlockSpec((None, d, nq), at_layer, pipeline_mode=pl.Buffered(1)),
                  pl.BlockSpec((1, nm, nkv), at_layer),
                  pl.BlockSpec((None, nq, d), at_layer, pipeline_mode=pl.Buffered(1)),
                  pl.BlockSpec((1, d), fixed)],
        out_specs=[pl.BlockSpec((tm, d), row), pl.BlockSpec((tm, d), row)],
        out_shape=[jax.ShapeDtypeStruct((m, d), F32),
                   jax.ShapeDtypeStruct((m, d), BF16)],
        scratch_shapes=[pltpu.VMEM((d, nq), BF16), pltpu.VMEM((nq, d), BF16)],
        compiler_params=_params(1), name="mem_attn",
    )(h, x, wq, kv_all, wo, g.reshape(1, d))


def _conv3(u_ref, r0, nr, cols, cw, cb):
    a = SUBLANES + r0
    return (cb + cw[2:3, :] * u_ref[a:a + nr, cols] + cw[1:2, :] * u_ref[a - 1:a - 1 + nr, cols]
            + cw[0:1, :] * u_ref[a - 2:a - 2 + nr, cols])


def _silu_gate(gate, up):
    return gate * (1.0 / (1.0 + jnp.exp(-gate))) * up


def _ffn_in_kernel(h_ref, wg_ref, wu_ref, cwg_ref, cwu_ref, cbg_ref, cbu_ref,
                   o_ref, wgb_ref, wub_ref, raw_ref, *, tn, tc, ni, nt):
    hs = SUBLANES
    tm = h_ref.shape[0]
    t = pl.program_id(0)
    slot = t % 2
    i_cur = jnp.minimum(t, nt - 1) % ni
    cur = (raw_ref.at[slot, 0], raw_ref.at[slot, 1])
    prev = (raw_ref.at[1 - slot, 0], raw_ref.at[1 - slot, 1])

    @pl.when(t == 0)
    def _():
        for r in prev:
            r[...] = jnp.zeros(r.shape, F32)

    @pl.when(jnp.logical_and(i_cur == 0, t < nt))
    def _():
        _cast_rows(wgb_ref, wg_ref, 256)
        _cast_rows(wub_ref, wu_ref, 256)
        for r in cur:
            r[:hs, :] = jnp.zeros((hs, tn), F32)

    @pl.when(i_cur > 0)
    def _():
        for r, p in zip(cur, prev):
            r[:hs, :] = p[tm:tm + hs, :]

    def epilogue(r0, nr):
        tok = jnp.zeros((hs, LANES), F32)
        for c in range(tn // tc):
            cols = slice(c * tc, (c + 1) * tc)
            gate = _conv3(prev[0], r0, nr, cols, cwg_ref[:, cols], cbg_ref[:, cols])
            up = _conv3(prev[1], r0, nr, cols, cwu_ref[:, cols], cbu_ref[:, cols])
            g = _silu_gate(gate, up)
            o_ref[r0:r0 + nr, cols] = g.astype(o_ref.dtype)
            part = jnp.sum(g.reshape(nr // hs, hs, tc), axis=0)
            for l in range(tc // LANES):
                tok = tok + part[:, l * LANES:(l + 1) * LANES]
        return tok

    h = h_ref[...]
    q = tm // 4
    tok = epilogue(0, q)
    acc = jnp.dot(h, wgb_ref[...], preferred_element_type=F32)
    cur[0][hs:, :] = acc
    zero = (pltpu.bitcast(tok, jnp.uint32) >> 16) >> 16
    cur[0][hs:2 * hs, :LANES] = jnp.where(zero == 0, acc[:hs, :LANES], acc[hs:2 * hs, :LANES])
    epilogue(q, tm - q)
    cur[1][hs:, :] = jnp.dot(h, wub_ref[...], preferred_element_type=F32)


def _ffn_in(h, w_in, conv_w, conv_b, layer, tm=1024, tn=512, tc=256):
    m, k = h.shape
    dff = w_in.shape[2] // 2
    tm = min(tm, m)
    nj, ni = dff // tn, m // tm
    nt = nj * ni
    conv_b = conv_b.reshape(conv_b.shape[0], 1, -1)
    mm_tile = lambda t: jnp.minimum(t, nt - 1)
    ep_tile = lambda t: jnp.maximum(t - 1, 0)
    gate_w = lambda t: (layer, 0, mm_tile(t) // ni)
    up_w = lambda t: (layer, 0, nj + mm_tile(t) // ni)
    gate_c = lambda t: (layer, 0, ep_tile(t) // ni)
    up_c = lambda t: (layer, 0, nj + ep_tile(t) // ni)
    return pl.pallas_call(
        functools.partial(_ffn_in_kernel, tn=tn, tc=tc, ni=ni, nt=nt),
        grid=(nt + 1,),
        in_specs=[pl.BlockSpec((tm, k), lambda t: (mm_tile(t) % ni, 0)),
                  pl.BlockSpec((None, k, tn), gate_w),
                  pl.BlockSpec((None, k, tn), up_w),
                  pl.BlockSpec((None, CONV_W, tn), gate_c),
                  pl.BlockSpec((None, CONV_W, tn), up_c),
                  pl.BlockSpec((None, 1, tn), gate_c),
                  pl.BlockSpec((None, 1, tn), up_c)],
        out_specs=pl.BlockSpec((tm, tn), lambda t: (ep_tile(t) % ni, ep_tile(t) // ni)),
        out_shape=jax.ShapeDtypeStruct((m, dff), BF16),
        scratch_shapes=[pltpu.VMEM((k, tn), BF16), pltpu.VMEM((k, tn), BF16),
                        pltpu.VMEM((2, 2, SUBLANES + tm, tn), F32)],
        compiler_params=_params(1), name="ffn_in",
    )(h, w_in, w_in, conv_w, conv_w, conv_b, conv_b)


def _rope_slots(t):
    half = ROPE_DIM // 2
    z = jnp.zeros(t.shape[:-1] + (LANES // 2 - half,), t.dtype)
    return jnp.concatenate([t[..., :half], z, t[..., half:], z], axis=-1)


def _rope_tables(positions):
    inv_freq = ROPE_THETA ** (-jnp.arange(0, ROPE_DIM, 2, dtype=F32) / ROPE_DIM)
    ang = positions.astype(F32)[:, None] * inv_freq
    cos, sin = jnp.cos(ang), jnp.sin(ang)
    return (_rope_slots(jnp.concatenate([cos, cos], axis=-1)),
            _rope_slots(jnp.concatenate([-sin, sin], axis=-1)))


def kernel(x, mem, positions, norm_mix, norm_mem_q, norm_mem_kv, norm_ffn, norm_final, mla_w_down, mla_q_norm, mla_w_uq, mla_kv_norm, mla_w_ukv, mla_w_o, sb_w_qkv, sb_w_o, mem_w_q, mem_w_kv, mem_w_o, ffn_w_in, ffn_conv_w, ffn_conv_b, ffn_w_out):
    assert x.shape[0] == 1, "batch is fixed to 1"
    xs = x[0]
    cos, sin = _rope_tables(positions[0])
    mem_kv = _mem_kv(mem[0], norm_mem_kv, mem_w_kv)
    mla_scale = LOG2E / math.sqrt(NOPE_DIM + ROPE_DIM)
    sb_scale = LOG2E / math.sqrt(SB_HEAD_DIM)

    h = _rmsnorm(xs, norm_mix[0], BF16)
    for i in range(DEPTH):
        j = i // N_MIXERS
        if i % N_MIXERS == 0:
            nlat = Q_LORA + KV_LORA
            w_down = jnp.concatenate(
                [mla_w_down[j][:, :nlat], _rope_slots(mla_w_down[j][:, nlat:])], axis=1)
            c_q, c_kv, k_rope = _mla_down(h, w_down, mla_q_norm[j], mla_kv_norm[j], cos, sin)
            w_uq = mla_w_uq[j].reshape(Q_LORA, MLA_HEADS, NOPE_DIM + ROPE_DIM)
            w_qn = w_uq[:, :, :NOPE_DIM].reshape(Q_LORA, MLA_HEADS * NOPE_DIM)
            w_qr = _rope_slots(w_uq[:, :, NOPE_DIM:]).reshape(Q_LORA, MLA_HEADS * LANES)
            qn = _mm(c_q, w_qn[None], 0, tm=2048, tn=1024, out_dtype=BF16, mode="scale",
                     scale=mla_scale)
            qr = _mm(c_q, w_qr[None], 0, tm=2048, tn=1024, out_dtype=BF16, mode="rope",
                     scale=mla_scale, cos=cos, sin=sin)
            kv = _mm(c_kv, mla_w_ukv, j, tm=2048, tn=1024, out_dtype=BF16)
            o = _mla_attn(qn, qr, kv, k_rope)
            w_o = mla_w_o
        else:
            qkv = _mm(h, sb_w_qkv, j, tm=1024, tn=1024, out_dtype=BF16, mode="scale_cols",
                      scale=sb_scale, scale_cols=SB_HEADS * SB_HEAD_DIM)
            o = _sb_attn(qkv)
            w_o = sb_w_o
        xs, hq = _proj_res_norm(o, w_o, j, xs, norm_mem_q[i])
        xs, hf = _mem_attn(hq, xs, mem_w_q, mem_kv, i, mem_w_o, norm_ffn[i])
        g = _ffn_in(hf, ffn_w_in, ffn_conv_w, ffn_conv_b, i)
        xs = _mm(g, ffn_w_out, i, tm=512, tn=512, out_dtype=F32, mode="residual", res=xs)
        last = i == DEPTH - 1
        h = _rmsnorm(xs, norm_final if last else norm_mix[i + 1], F32 if last else BF16)
    return h[None]
```

```python
import functools
import math

import jax
import jax.numpy as jnp
from jax import lax
from jax.experimental import pallas as pl
from jax.experimental.pallas import tpu as pltpu

F32 = jnp.float32
BF16 = jnp.bfloat16

D_MODEL = 2048
DEPTH = 4
CHUNK = 64
N_MIXERS = 2
MLA_HEADS = 16
Q_LORA = 512
KV_LORA = 512
NOPE_DIM = 128
ROPE_DIM = 64
V_DIM = 128
ROPE_THETA = 10000.0
SB_HEADS = 16
SB_HEAD_DIM = 128
N_MEM = 256
MEM_HEADS = 4
MEM_HEAD_DIM = 128
D_FF = 5632
CONV_W = 3
EPS = 1e-6

LANES = 128
SUBLANES = 8
VMEM_LIMIT = 56 * 1024 * 1024
NEG_BIG = -1e30
LOG2E = 1.4426950408889634
SB_STOP = 120.0


def _params(n_grid, flags=None):
    return pltpu.CompilerParams(
        dimension_semantics=("arbitrary",) * n_grid,
        vmem_limit_bytes=VMEM_LIMIT, flags=flags)


def _cast_rows(dst_ref, src_ref, rows):
    n = src_ref.shape[0]
    rows = min(rows, n)

    def body(r, c):
        sl = pl.ds(pl.multiple_of(r * rows, rows), rows)
        dst_ref[sl, :] = src_ref[sl, :].astype(dst_ref.dtype)
        return c

    lax.fori_loop(0, n // rows, body, 0)


def _rms(x, g):
    ms = jnp.mean(x * x, axis=-1, keepdims=True)
    return x * lax.rsqrt(ms + EPS) * g


def _rmsnorm_kernel(x_ref, g_ref, o_ref):
    o_ref[...] = _rms(x_ref[...], g_ref[...]).astype(o_ref.dtype)


def _rmsnorm(x, g, out_dtype, tm=1024):
    m, d = x.shape
    tm = min(tm, m)
    return pl.pallas_call(
        _rmsnorm_kernel,
        grid=(m // tm,),
        in_specs=[pl.BlockSpec((tm, d), lambda i: (i, 0)),
                  pl.BlockSpec((1, d), lambda i: (0, 0))],
        out_specs=pl.BlockSpec((tm, d), lambda i: (i, 0)),
        out_shape=jax.ShapeDtypeStruct((m, d), out_dtype),
        compiler_params=_params(1), name="rmsnorm",
    )(x, g.reshape(1, d))


def _rope_heads(acc, cos, sin):
    outs = []
    for h in range(acc.shape[1] // LANES):
        a = acc[:, h * LANES:(h + 1) * LANES]
        outs.append(a * cos + pltpu.roll(a, LANES // 2, axis=1) * sin)
    return outs[0] if len(outs) == 1 else jnp.concatenate(outs, axis=1)


def _mm_kernel(*refs, mode, scale, scale_cols, tn):
    if mode == "rope":
        x_ref, w_ref, cos_ref, sin_ref, o_ref, wb_ref = refs
    elif mode == "residual":
        x_ref, w_ref, res_ref, o_ref, wb_ref = refs
    else:
        x_ref, w_ref, o_ref, wb_ref = refs

    @pl.when(pl.program_id(1) == 0)
    def _():
        _cast_rows(wb_ref, w_ref, 256)

    acc = jnp.dot(x_ref[...], wb_ref[...], preferred_element_type=F32)
    if mode == "rope":
        acc = _rope_heads(acc, cos_ref[...], sin_ref[...]) * scale
    elif mode == "residual":
        acc = acc + res_ref[...]
    elif mode == "scale":
        acc = acc * scale
    elif mode == "scale_cols":
        s = jnp.where(pl.program_id(0) * tn < scale_cols, scale, 1.0).astype(F32)
        acc = acc * s
    o_ref[...] = acc.astype(o_ref.dtype)


def _mm(x, w, layer, *, tm, tn, out_dtype, mode="plain", scale=1.0, scale_cols=0,
        cos=None, sin=None, res=None):
    m, k = x.shape
    n = w.shape[2]
    tm, tn = min(tm, m), min(tn, n)
    in_specs = [pl.BlockSpec((tm, k), lambda j, i: (i, 0)),
                pl.BlockSpec((None, k, tn), lambda j, i: (layer, 0, j))]
    args = [x, w]
    if mode == "rope":
        in_specs += [pl.BlockSpec((tm, LANES), lambda j, i: (i, 0))] * 2
        args += [cos, sin]
    elif mode == "residual":
        in_specs += [pl.BlockSpec((tm, tn), lambda j, i: (i, j))]
        args += [res]
    return pl.pallas_call(
        functools.partial(_mm_kernel, mode=mode, scale=scale, scale_cols=scale_cols, tn=tn),
        grid=(n // tn, m // tm),
        in_specs=in_specs,
        out_specs=pl.BlockSpec((tm, tn), lambda j, i: (i, j)),
        out_shape=jax.ShapeDtypeStruct((m, n), out_dtype),
        scratch_shapes=[pltpu.VMEM((k, tn), BF16)],
        compiler_params=_params(2), name="mm_" + mode,
    )(*args)


def _mla_down_kernel(h_ref, w_ref, gq_ref, gkv_ref, cos_ref, sin_ref,
                     cq_ref, ckv_ref, kr_ref, wb_ref):
    @pl.when(pl.program_id(0) == 0)
    def _():
        _cast_rows(wb_ref, w_ref, 256)

    acc = jnp.dot(h_ref[...], wb_ref[...], preferred_element_type=F32)
    cq_ref[...] = _rms(acc[:, :Q_LORA], gq_ref[...]).astype(BF16)
    ckv_ref[...] = _rms(acc[:, Q_LORA:Q_LORA + KV_LORA], gkv_ref[...]).astype(BF16)
    kr = acc[:, Q_LORA + KV_LORA:]
    kr_ref[...] = _rope_heads(kr, cos_ref[...], sin_ref[...]).astype(BF16)


def _mla_down(h, w, gq, gkv, cos, sin, tm=1024):
    m, k = h.shape
    n = w.shape[1]
    tm = min(tm, m)
    row = lambda i: (i, 0)
    fixed = lambda i: (0, 0)
    return pl.pallas_call(
        _mla_down_kernel,
        grid=(m // tm,),
        in_specs=[pl.BlockSpec((tm, k), row),
                  pl.BlockSpec((k, n), fixed),
                  pl.BlockSpec((1, Q_LORA), fixed),
                  pl.BlockSpec((1, KV_LORA), fixed),
                  pl.BlockSpec((tm, LANES), row),
                  pl.BlockSpec((tm, LANES), row)],
        out_specs=[pl.BlockSpec((tm, Q_LORA), row),
                   pl.BlockSpec((tm, KV_LORA), row),
                   pl.BlockSpec((tm, LANES), row)],
        out_shape=[jax.ShapeDtypeStruct((m, Q_LORA), BF16),
                   jax.ShapeDtypeStruct((m, KV_LORA), BF16),
                   jax.ShapeDtypeStruct((m, LANES), BF16)],
        scratch_shapes=[pltpu.VMEM((k, n), BF16)],
        compiler_params=_params(1), name="mla_down",
    )(h, w, gq.reshape(1, -1), gkv.reshape(1, -1), cos, sin)


def _mla_attn_kernel(qn_ref, qr_ref, kv_ref, kr_ref, o_ref,
                     s_ref, p_ref, m_ref, l_ref, acc_ref, *, tq, tk, nh, rg):
    i = pl.program_id(1)
    hd = NOPE_DIM
    m_ref[...] = jnp.full(m_ref.shape, NEG_BIG, F32)
    l_ref[...] = jnp.zeros_like(l_ref)
    acc_ref[...] = jnp.zeros_like(acc_ref)

    def step(kb, diag):
        rows = pl.ds(pl.multiple_of(kb * tk, tk), tk)
        kr = kr_ref[rows, :]
        for h in range(nh):
            q = jnp.concatenate([qn_ref[:, h * hd:(h + 1) * hd],
                                 qr_ref[:, h * hd:(h + 1) * hd]], axis=1)
            k = jnp.concatenate([kv_ref[rows, 2 * h * hd:(2 * h + 1) * hd], kr], axis=1)
            s = lax.dot_general(q, k, (((1,), (1,)), ((), ())),
                                preferred_element_type=F32)
            if diag is not None:
                qc = lax.broadcasted_iota(jnp.int32, (tq, tk), 0) // CHUNK
                kc = (lax.broadcasted_iota(jnp.int32, (tq, tk), 1) + diag * tk) // CHUNK
                s = jnp.where(kc <= qc, s, NEG_BIG)
            s_ref[h] = s
        nt = tk // LANES
        alphas = []
        for h in range(nh):
            m_tile = s_ref[h, :, 0:LANES]
            for c in range(1, nt):
                m_tile = jnp.maximum(m_tile, s_ref[h, :, c * LANES:(c + 1) * LANES])
            m_prev = m_ref[h]
            m_rep = jnp.maximum(m_prev, jnp.max(m_tile, axis=1, keepdims=True))
            m_ref[h] = m_rep
            alpha = jnp.exp2(m_prev - m_rep)
            alphas.append(alpha)
            for r in range(tq // rg):
                rs = slice(r * rg, (r + 1) * rg)
                part = None
                for c in range(nt):
                    cs = slice(c * LANES, (c + 1) * LANES)
                    p = jnp.exp2(s_ref[h, rs, cs] - m_rep[rs])
                    p_ref[h, rs, cs] = p.astype(BF16)
                    part = p if part is None else part + p
                l_ref[h, rs, :] = alpha[rs] * l_ref[h, rs, :] + part
        for h in range(nh):
            v = kv_ref[rows, (2 * h + 1) * hd:(2 * h + 2) * hd]
            acc_ref[h] = alphas[h] * acc_ref[h] + jnp.dot(p_ref[h], v, preferred_element_type=F32)

    def body(kp, c):
        for u in range(4):
            step(4 * kp + u, None)
        return c

    nd = tq // tk
    n_full = i * nd
    lax.fori_loop(0, n_full // 4, body, 0)
    rem = n_full % 4

    @pl.when(rem >= 2)
    def _():
        step(n_full - rem, None)
        step(n_full - rem + 1, None)

    @pl.when(rem % 2 == 1)
    def _():
        step(n_full - 1, None)

    for d in range(nd):
        step(n_full + d, d)
    for h in range(nh):
        l = jnp.sum(l_ref[h], axis=1, keepdims=True)
        o_ref[:, h * hd:(h + 1) * hd] = (acc_ref[h] / l).astype(o_ref.dtype)


def _mla_attn(qn, qr, kv, kr, tq=512, tk=512, nh=4, rg=32):
    s = qn.shape[0]
    tq, tk = min(tq, s), min(tk, s)
    assert CHUNK % rg == 0 and tk % CHUNK == 0 and tq % tk == 0
    return pl.pallas_call(
        functools.partial(_mla_attn_kernel, tq=tq, tk=tk, nh=nh, rg=rg),
        scratch_shapes=[pltpu.VMEM((nh, tq, tk), F32), pltpu.VMEM((nh, tq, tk), BF16),
                        pltpu.VMEM((nh, tq, LANES), F32), pltpu.VMEM((nh, tq, LANES), F32),
                        pltpu.VMEM((nh, tq, V_DIM), F32)],
        grid=(MLA_HEADS // nh, s // tq),
        in_specs=[pl.BlockSpec((tq, nh * NOPE_DIM), lambda g, i: (i, g)),
                  pl.BlockSpec((tq, nh * LANES), lambda g, i: (i, g)),
                  pl.BlockSpec((s, nh * (NOPE_DIM + V_DIM)), lambda g, i: (0, g)),
                  pl.BlockSpec((s, LANES), lambda g, i: (0, 0))],
        out_specs=pl.BlockSpec((tq, nh * V_DIM), lambda g, i: (i, g)),
        out_shape=jax.ShapeDtypeStruct((s, MLA_HEADS * V_DIM), BF16),
        compiler_params=_params(2), name="mla_attn",
    )(qn, qr, kv, kr)


def _sb_attn_kernel(q_ref, k_ref, v_ref, o_ref, carry_ref, acc_ref, live_ref, *, tq, nh):
    i = pl.program_id(1)
    hd = SB_HEAD_DIM
    row = lax.broadcasted_iota(jnp.int32, (tq, tq), 0)
    col = lax.broadcasted_iota(jnp.int32, (tq, tq), 1)
    upper = jnp.where(row > col, 1.0, 0.0).astype(BF16)
    carry_ref[...] = jnp.zeros_like(carry_ref)
    acc_ref[...] = jnp.zeros_like(acc_ref)

    def step(kb, diagonal):
        rows = pl.ds(pl.multiple_of(kb * tq, tq), tq)
        heads = range(nh)
        hcols = [slice(h * hd, (h + 1) * hd) for h in heads]
        z = [lax.dot_general(q_ref[:, hcols[h]], k_ref[rows, hcols[h]], (((1,), (1,)), ((), ())),
                             preferred_element_type=F32) for h in heads]
        sp = [jnp.maximum(z[h], 0.0) + jnp.log2(1.0 + jnp.exp2(-jnp.abs(z[h]))) for h in heads]
        lk = [jnp.where(col < row, -sp[h], 0.0) if diagonal else -sp[h] for h in heads]
        later = []
        for h in heads:
            lk_hi = lk[h].astype(BF16)
            lk_lo = (lk[h] - lk_hi.astype(F32)).astype(BF16)
            later.append(jnp.dot(lk_hi, upper, preferred_element_type=F32)
                         + jnp.dot(lk_lo, upper, preferred_element_type=F32))
        top = None
        for h in heads:
            carry = carry_ref[h]
            a = jnp.exp2(z[h] - sp[h] + later[h] + jnp.tile(carry, (1, tq // LANES)))
            if diagonal:
                a = jnp.where(col < row, a, 0.0)
            acc_ref[:, hcols[h]] += jnp.dot(a.astype(BF16), v_ref[rows, hcols[h]],
                                            preferred_element_type=F32)
            carry = carry + jnp.sum(lk[h], axis=1, keepdims=True)
            carry_ref[h] = carry
            top = carry if top is None else jnp.maximum(top, carry)
        return jnp.max(top) > -SB_STOP * LOG2E

    def cond(state):
        kb, live = state
        return jnp.logical_and(kb >= 0, live)

    def body(state):
        kb, _ = state
        return kb - 1, step(kb, False)

    @pl.when(i == 0)
    def _():
        step(0, True)
        live_ref[0] = 0

    @pl.when(i > 0)
    def _():
        step(i, True)
        live_ref[0] = step(i - 1, False).astype(jnp.int32)

    lax.while_loop(cond, body, (i - 2, live_ref[0] > 0))
    o_ref[...] = acc_ref[...].astype(o_ref.dtype)


def _sb_attn(qkv, tq=256, nh=4):
    s = qkv.shape[0]
    tq = min(tq, s)
    ng = SB_HEADS // nh
    w = nh * SB_HEAD_DIM
    return pl.pallas_call(
        functools.partial(_sb_attn_kernel, tq=tq, nh=nh),
        grid=(ng, s // tq),
        in_specs=[pl.BlockSpec((tq, w), lambda g, i: (i, g)),
                  pl.BlockSpec((s, w), lambda g, i: (0, ng + g)),
                  pl.BlockSpec((s, w), lambda g, i: (0, 2 * ng + g))],
        out_specs=pl.BlockSpec((tq, w), lambda g, i: (i, g)),
        out_shape=jax.ShapeDtypeStruct((s, SB_HEADS * SB_HEAD_DIM), BF16),
        scratch_shapes=[pltpu.VMEM((nh, tq, LANES), F32), pltpu.VMEM((tq, w), F32),
                        pltpu.SMEM((1,), jnp.int32)],
        compiler_params=_params(2), name="sb_attn",
    )(qkv, qkv, qkv)


def _proj_res_norm_kernel(a_ref, w_ref, res_ref, g_ref, x_ref, h_ref, wb_ref):
    @pl.when(pl.program_id(0) == 0)
    def _():
        _cast_rows(wb_ref, w_ref, 256)

    x = res_ref[...] + jnp.dot(a_ref[...], wb_ref[...], preferred_element_type=F32)
    x_ref[...] = x
    h_ref[...] = _rms(x, g_ref[...]).astype(BF16)


def _proj_res_norm(a, w, layer, res, g, tm=512):
    m, k = a.shape
    n = w.shape[2]
    tm = min(tm, m)
    row = lambda i: (i, 0)
    fixed = lambda i: (0, 0)
    return pl.pallas_call(
        _proj_res_norm_kernel,
        grid=(m // tm,),
        in_specs=[pl.BlockSpec((tm, k), row),
                  pl.BlockSpec((None, k, n), lambda i: (layer, 0, 0),
                               pipeline_mode=pl.Buffered(1)),
                  pl.BlockSpec((tm, n), row),
                  pl.BlockSpec((1, n), fixed)],
        out_specs=[pl.BlockSpec((tm, n), row), pl.BlockSpec((tm, n), row)],
        out_shape=[jax.ShapeDtypeStruct((m, n), F32),
                   jax.ShapeDtypeStruct((m, n), BF16)],
        scratch_shapes=[pltpu.VMEM((k, n), BF16)],
        compiler_params=_params(1), name="proj_res_norm",
    )(a, w, res, g.reshape(1, n))


def _mem_kv_kernel(mem_ref, g_ref, w_ref, o_ref):
    hm = _rms(mem_ref[...], g_ref[0]).astype(BF16)
    half = w_ref.shape[2] // 2
    for c in range(2):
        w = w_ref[0, :, c * half:(c + 1) * half].astype(BF16)
        o_ref[0, :, c * half:(c + 1) * half] = jnp.dot(
            hm, w, preferred_element_type=F32).astype(BF16)


def _mem_kv(mem, g, w):
    depth, d, n = w.shape
    nm = mem.shape[0]
    return pl.pallas_call(
        _mem_kv_kernel,
        grid=(depth,),
        in_specs=[pl.BlockSpec((nm, d), lambda l: (0, 0)),
                  pl.BlockSpec((1, 1, d), lambda l: (l, 0, 0)),
                  pl.BlockSpec((1, d, n), lambda l: (l, 0, 0))],
        out_specs=pl.BlockSpec((1, nm, n), lambda l: (l, 0, 0)),
        out_shape=jax.ShapeDtypeStruct((depth, nm, n), BF16),
        compiler_params=_params(1), name="mem_kv",
    )(mem, g.reshape(depth, 1, d), w)


def _mem_attn_kernel(h_ref, x_ref, wq_ref, kv_ref, wo_ref, g_ref,
                     xo_ref, ho_ref, wqb_ref, wob_ref):
    @pl.when(pl.program_id(0) == 0)
    def _():
        _cast_rows(wqb_ref, wq_ref, 256)
        _cast_rows(wob_ref, wo_ref, 256)

    hd = MEM_HEAD_DIM
    nk = MEM_HEADS * hd
    nm = kv_ref.shape[1]
    heads = range(MEM_HEADS)
    scale = LOG2E / math.sqrt(hd)
    q = (jnp.dot(h_ref[...], wqb_ref[...], preferred_element_type=F32) * scale).astype(BF16)
    s = [lax.dot_general(q[:, h * hd:(h + 1) * hd], kv_ref[0, :, h * hd:(h + 1) * hd],
                         (((1,), (1,)), ((), ())), preferred_element_type=F32)
         for h in heads]
    p = [jnp.exp2(s[h] - jnp.max(s[h], axis=1, keepdims=True)).astype(BF16) for h in heads]
    ones = jnp.ones((nm, hd), BF16)
    o = []
    for h in heads:
        v1 = jnp.concatenate([kv_ref[0, :, nk + h * hd:nk + (h + 1) * hd], ones], axis=1)
        ol = jnp.dot(p[h], v1, preferred_element_type=F32)
        o.append((ol[:, :hd] / ol[:, hd:]).astype(BF16))
    x = x_ref[...] + jnp.dot(jnp.concatenate(o, axis=1), wob_ref[...], preferred_element_type=F32)
    xo_ref[...] = x
    ho_ref[...] = _rms(x, g_ref[...]).astype(BF16)


def _mem_attn(h, x, wq, kv_all, layer, wo, g, tm=512):
    m, d = h.shape
    nq = wq.shape[2]
    tm = min(tm, m)
    row = lambda i: (i, 0)
    fixed = lambda i: (0, 0)
    at_layer = lambda i: (layer, 0, 0)
    nm, nkv = kv_all.shape[1:]
    return pl.pallas_call(
        _mem_attn_kernel,
        grid=(m // tm,),
        in_specs=[pl.BlockSpec((tm, d), row),
                  pl.BlockSpec((tm, d), row),
                  pl.BlockSpec((None, d, nq), at_layer, pipeline_mode=pl.Buffered(1)),
                  pl.BlockSpec((1, nm, nkv), at_layer),
                  pl.BlockSpec((None, nq, d), at_layer, pipeline_mode=pl.Buffered(1)),
                  pl.BlockSpec((1, d), fixed)],
        out_specs=[pl.BlockSpec((tm, d), row), pl.BlockSpec((tm, d), row)],
        out_shape=[jax.ShapeDtypeStruct((m, d), F32),
                   jax.ShapeDtypeStruct((m, d), BF16)],
        scratch_shapes=[pltpu.VMEM((d, nq), BF16), pltpu.VMEM((nq, d), BF16)],
        compiler_params=_params(1), name="mem_attn",
    )(h, x, wq, kv_all, wo, g.reshape(1, d))


def _conv3(u_ref, r0, nr, cols, cw, cb):
    a = SUBLANES + r0
    return (cb + cw[2:3, :] * u_ref[a:a + nr, cols] + cw[1:2, :] * u_ref[a - 1:a - 1 + nr, cols]
            + cw[0:1, :] * u_ref[a - 2:a - 2 + nr, cols])


def _silu_gate(gate, up):
    return gate * (1.0 / (1.0 + jnp.exp(-gate))) * up


def _ffn_in_kernel(h_ref, wg_ref, wu_ref, cwg_ref, cwu_ref, cbg_ref, cbu_ref,
                   o_ref, wgb_ref, wub_ref, raw_ref, *, tn, tc, ni, nt):
    hs = SUBLANES
    tm = h_ref.shape[0]
    t = pl.program_id(0)
    slot = t % 2
    i_cur = jnp.minimum(t, nt - 1) % ni
    cur = (raw_ref.at[slot, 0], raw_ref.at[slot, 1])
    prev = (raw_ref.at[1 - slot, 0], raw_ref.at[1 - slot, 1])

    @pl.when(t == 0)
    def _():
        for r in prev:
            r[...] = jnp.zeros(r.shape, F32)

    @pl.when(jnp.logical_and(i_cur == 0, t < nt))
    def _():
        _cast_rows(wgb_ref, wg_ref, 256)
        _cast_rows(wub_ref, wu_ref, 256)
        for r in cur:
            r[:hs, :] = jnp.zeros((hs, tn), F32)

    @pl.when(i_cur > 0)
    def _():
        for r, p in zip(cur, prev):
            r[:hs, :] = p[tm:tm + hs, :]

    def epilogue(r0, nr):
        tok = jnp.zeros((hs, LANES), F32)
        for c in range(tn // tc):
            cols = slice(c * tc, (c + 1) * tc)
            gate = _conv3(prev[0], r0, nr, cols, cwg_ref[:, cols], cbg_ref[:, cols])
            up = _conv3(prev[1], r0, nr, cols, cwu_ref[:, cols], cbu_ref[:, cols])
            g = _silu_gate(gate, up)
            o_ref[r0:r0 + nr, cols] = g.astype(o_ref.dtype)
            part = jnp.sum(g.reshape(nr // hs, hs, tc), axis=0)
            for l in range(tc // LANES):
                tok = tok + part[:, l * LANES:(l + 1) * LANES]
        return tok

    h = h_ref[...]
    q = tm // 4
    tok = epilogue(0, q)
    acc = jnp.dot(h, wgb_ref[...], preferred_element_type=F32)
    cur[0][hs:, :] = acc
    zero = (pltpu.bitcast(tok, jnp.uint32) >> 16) >> 16
    cur[0][hs:2 * hs, :LANES] = jnp.where(zero == 0, acc[:hs, :LANES], acc[hs:2 * hs, :LANES])
    epilogue(q, tm - q)
    cur[1][hs:, :] = jnp.dot(h, wub_ref[...], preferred_element_type=F32)


def _ffn_in(h, w_in, conv_w, conv_b, layer, tm=1024, tn=512, tc=256):
    m, k = h.shape
    dff = w_in.shape[2] // 2
    tm = min(tm, m)
    nj, ni = dff // tn, m // tm
    nt = nj * ni
    conv_b = conv_b.reshape(conv_b.shape[0], 1, -1)
    mm_tile = lambda t: jnp.minimum(t, nt - 1)
    ep_tile = lambda t: jnp.maximum(t - 1, 0)
    gate_w = lambda t: (layer, 0, mm_tile(t) // ni)
    up_w = lambda t: (layer, 0, nj + mm_tile(t) // ni)
    gate_c = lambda t: (layer, 0, ep_tile(t) // ni)
    up_c = lambda t: (layer, 0, nj + ep_tile(t) // ni)
    return pl.pallas_call(
        functools.partial(_ffn_in_kernel, tn=tn, tc=tc, ni=ni, nt=nt),
        grid=(nt + 1,),
        in_specs=[pl.BlockSpec((tm, k), lambda t: (mm_tile(t) % ni, 0)),
                  pl.BlockSpec((None, k, tn), gate_w),
                  pl.BlockSpec((None, k, tn), up_w),
                  pl.BlockSpec((None, CONV_W, tn), gate_c),
                  pl.BlockSpec((None, CONV_W, tn), up_c),
                  pl.BlockSpec((None, 1, tn), gate_c),
                  pl.BlockSpec((None, 1, tn), up_c)],
        out_specs=pl.BlockSpec((tm, tn), lambda t: (ep_tile(t) % ni, ep_tile(t) // ni)),
        out_shape=jax.ShapeDtypeStruct((m, dff), BF16),
        scratch_shapes=[pltpu.VMEM((k, tn), BF16), pltpu.VMEM((k, tn), BF16),
                        pltpu.VMEM((2, 2, SUBLANES + tm, tn), F32)],
        compiler_params=_params(1), name="ffn_in",
    )(h, w_in, w_in, conv_w, conv_w, conv_b, conv_b)


def _rope_slots(t):
    half = ROPE_DIM // 2
    z = jnp.zeros(t.shape[:-1] + (LANES // 2 - half,), t.dtype)
    return jnp.concatenate([t[..., :half], z, t[..., half:], z], axis=-1)


def _rope_tables(positions):
    inv_freq = ROPE_THETA ** (-jnp.arange(0, ROPE_DIM, 2, dtype=F32) / ROPE_DIM)
    ang = positions.astype(F32)[:, None] * inv_freq
    cos, sin = jnp.cos(ang), jnp.sin(ang)
    return (_rope_slots(jnp.concatenate([cos, cos], axis=-1)),
            _rope_slots(jnp.concatenate([-sin, sin], axis=-1)))


def kernel(x, mem, positions, norm_mix, norm_mem_q, norm_mem_kv, norm_ffn, norm_final, mla_w_down, mla_q_norm, mla_w_uq, mla_kv_norm, mla_w_ukv, mla_w_o, sb_w_qkv, sb_w_o, mem_w_q, mem_w_kv, mem_w_o, ffn_w_in, ffn_conv_w, ffn_conv_b, ffn_w_out):
    assert x.shape[0] == 1, "batch is fixed to 1"
    xs = x[0]
    cos, sin = _rope_tables(positions[0])
    mem_kv = _mem_kv(mem[0], norm_mem_kv, mem_w_kv)
    mla_scale = LOG2E / math.sqrt(NOPE_DIM + ROPE_DIM)
    sb_scale = LOG2E / math.sqrt(SB_HEAD_DIM)

    h = _rmsnorm(xs, norm_mix[0], BF16)
    for i in range(DEPTH):
        j = i // N_MIXERS
        if i % N_MIXERS == 0:
            nlat = Q_LORA + KV_LORA
            w_down = jnp.concatenate(
                [mla_w_down[j][:, :nlat], _rope_slots(mla_w_down[j][:, nlat:])], axis=1)
            c_q, c_kv, k_rope = _mla_down(h, w_down, mla_q_norm[j], mla_kv_norm[j], cos, sin)
            w_uq = mla_w_uq[j].reshape(Q_LORA, MLA_HEADS, NOPE_DIM + ROPE_DIM)
            w_qn = w_uq[:, :, :NOPE_DIM].reshape(Q_LORA, MLA_HEADS * NOPE_DIM)
            w_qr = _rope_slots(w_uq[:, :, NOPE_DIM:]).reshape(Q_LORA, MLA_HEADS * LANES)
            qn = _mm(c_q, w_qn[None], 0, tm=2048, tn=1024, out_dtype=BF16, mode="scale",
                     scale=mla_scale)
            qr = _mm(c_q, w_qr[None], 0, tm=2048, tn=1024, out_dtype=BF16, mode="rope",
                     scale=mla_scale, cos=cos, sin=sin)
            kv = _mm(c_kv, mla_w_ukv, j, tm=2048, tn=1024, out_dtype=BF16)
            o = _mla_attn(qn, qr, kv, k_rope)
            w_o = mla_w_o
        else:
            qkv = _mm(h, sb_w_qkv, j, tm=1024, tn=1024, out_dtype=BF16, mode="scale_cols",
                      scale=sb_scale, scale_cols=SB_HEADS * SB_HEAD_DIM)
            o = _sb_attn(qkv)
            w_o = sb_w_o
        xs, hq = _proj_res_norm(o, w_o, j, xs, norm_mem_q[i])
        xs, hf = _mem_attn(hq, xs, mem_w_q, mem_kv, i, mem_w_o, norm_ffn[i])
        g = _ffn_in(hf, ffn_w_in, ffn_conv_w, ffn_conv_b, i)
        xs = _mm(g, ffn_w_out, i, tm=512, tn=512, out_dtype=F32, mode="residual", res=xs)
        last = i == DEPTH - 1
        h = _rmsnorm(xs, norm_final if last else norm_mix[i + 1], F32 if last else BF16)
    return h[None]
```

```python
import functools
import math

import jax
import jax.numpy as jnp
from jax import lax
from jax.experimental import pallas as pl
from jax.experimental.pallas import tpu as pltpu

F32 = jnp.float32
BF16 = jnp.bfloat16

D_MODEL = 2048
DEPTH = 4
CHUNK = 64
N_MIXERS = 2
MLA_HEADS = 16
Q_LORA = 512
KV_LORA = 512
NOPE_DIM = 128
ROPE_DIM = 64
V_DIM = 128
ROPE_THETA = 10000.0
SB_HEADS = 16
SB_HEAD_DIM = 128
N_MEM = 256
MEM_HEADS = 4
MEM_HEAD_DIM = 128
D_FF = 5632
CONV_W = 3
EPS = 1e-6

LANES = 128
SUBLANES = 8
VMEM_LIMIT = 56 * 1024 * 1024
NEG_BIG = -1e30
LOG2E = 1.4426950408889634
SB_STOP = 120.0


def _params(n_grid, flags=None):
    return pltpu.CompilerParams(
        dimension_semantics=("arbitrary",) * n_grid,
        vmem_limit_bytes=VMEM_LIMIT, flags=flags)


def _cast_rows(dst_ref, src_ref, rows):
    n = src_ref.shape[0]
    rows = min(rows, n)

    def body(r, c):
        sl = pl.ds(pl.multiple_of(r * rows, rows), rows)
        dst_ref[sl, :] = src_ref[sl, :].astype(dst_ref.dtype)
        return c

    lax.fori_loop(0, n // rows, body, 0)


def _rms(x, g):
    ms = jnp.mean(x * x, axis=-1, keepdims=True)
    return x * lax.rsqrt(ms + EPS) * g


def _rmsnorm_kernel(x_ref, g_ref, o_ref):
    o_ref[...] = _rms(x_ref[...], g_ref[...]).astype(o_ref.dtype)


def _rmsnorm(x, g, out_dtype, tm=1024):
    m, d = x.shape
    tm = min(tm, m)
    return pl.pallas_call(
        _rmsnorm_kernel,
        grid=(m // tm,),
        in_specs=[pl.BlockSpec((tm, d), lambda i: (i, 0)),
                  pl.BlockSpec((1, d), lambda i: (0, 0))],
        out_specs=pl.BlockSpec((tm, d), lambda i: (i, 0)),
        out_shape=jax.ShapeDtypeStruct((m, d), out_dtype),
        compiler_params=_params(1), name="rmsnorm",
    )(x, g.reshape(1, d))


def _rope_heads(acc, cos, sin):
    outs = []
    for h in range(acc.shape[1] // LANES):
        a = acc[:, h * LANES:(h + 1) * LANES]
        outs.append(a * cos + pltpu.roll(a, LANES // 2, axis=1) * sin)
    return outs[0] if len(outs) == 1 else jnp.concatenate(outs, axis=1)


def _mm_kernel(*refs, mode, scale, scale_cols, tn):
    if mode == "rope":
        x_ref, w_ref, cos_ref, sin_ref, o_ref, wb_ref = refs
    elif mode == "residual":
        x_ref, w_ref, res_ref, o_ref, wb_ref = refs
    else:
        x_ref, w_ref, o_ref, wb_ref = refs

    @pl.when(pl.program_id(1) == 0)
    def _():
        _cast_rows(wb_ref, w_ref, 256)

    acc = jnp.dot(x_ref[...], wb_ref[...], preferred_element_type=F32)
    if mode == "rope":
        acc = _rope_heads(acc, cos_ref[...], sin_ref[...]) * scale
    elif mode == "residual":
        acc = acc + res_ref[...]
    elif mode == "scale":
        acc = acc * scale
    elif mode == "scale_cols":
        s = jnp.where(pl.program_id(0) * tn < scale_cols, scale, 1.0).astype(F32)
        acc = acc * s
    o_ref[...] = acc.astype(o_ref.dtype)


def _mm(x, w, layer, *, tm, tn, out_dtype, mode="plain", scale=1.0, scale_cols=0,
        cos=None, sin=None, res=None, single_buffer_w=False):
    m, k = x.shape
    n = w.shape[2]
    tm, tn = min(tm, m), min(tn, n)
    w_mode = dict(pipeline_mode=pl.Buffered(1)) if single_buffer_w else {}
    in_specs = [pl.BlockSpec((tm, k), lambda j, i: (i, 0)),
                pl.BlockSpec((None, k, tn), lambda j, i: (layer, 0, j), **w_mode)]
    args = [x, w]
    if mode == "rope":
        in_specs += [pl.BlockSpec((tm, LANES), lambda j, i: (i, 0))] * 2
        args += [cos, sin]
    elif mode == "residual":
        in_specs += [pl.BlockSpec((tm, tn), lambda j, i: (i, j))]
        args += [res]
    return pl.pallas_call(
        functools.partial(_mm_kernel, mode=mode, scale=scale, scale_cols=scale_cols, tn=tn),
        grid=(n // tn, m // tm),
        in_specs=in_specs,
        out_specs=pl.BlockSpec((tm, tn), lambda j, i: (i, j)),
        out_shape=jax.ShapeDtypeStruct((m, n), out_dtype),
        scratch_shapes=[pltpu.VMEM((k, tn), BF16)],
        compiler_params=_params(2), name="mm_" + mode,
    )(*args)


def _mla_down_kernel(h_ref, w_ref, gq_ref, gkv_ref, cos_ref, sin_ref,
                     cq_ref, ckv_ref, kr_ref, wb_ref):
    @pl.when(pl.program_id(0) == 0)
    def _():
        _cast_rows(wb_ref, w_ref, 256)

    acc = jnp.dot(h_ref[...], wb_ref[...], preferred_element_type=F32)
    cq_ref[...] = _rms(acc[:, :Q_LORA], gq_ref[...]).astype(BF16)
    ckv_ref[...] = _rms(acc[:, Q_LORA:Q_LORA + KV_LORA], gkv_ref[...]).astype(BF16)
    kr = acc[:, Q_LORA + KV_LORA:]
    kr_ref[...] = _rope_heads(kr, cos_ref[...], sin_ref[...]).astype(BF16)


def _mla_down(h, w, gq, gkv, cos, sin, tm=1024):
    m, k = h.shape
    n = w.shape[1]
    tm = min(tm, m)
    row = lambda i: (i, 0)
    fixed = lambda i: (0, 0)
    return pl.pallas_call(
        _mla_down_kernel,
        grid=(m // tm,),
        in_specs=[pl.BlockSpec((tm, k), row),
                  pl.BlockSpec((k, n), fixed),
                  pl.BlockSpec((1, Q_LORA), fixed),
                  pl.BlockSpec((1, KV_LORA), fixed),
                  pl.BlockSpec((tm, LANES), row),
                  pl.BlockSpec((tm, LANES), row)],
        out_specs=[pl.BlockSpec((tm, Q_LORA), row),
                   pl.BlockSpec((tm, KV_LORA), row),
                   pl.BlockSpec((tm, LANES), row)],
        out_shape=[jax.ShapeDtypeStruct((m, Q_LORA), BF16),
                   jax.ShapeDtypeStruct((m, KV_LORA), BF16),
                   jax.ShapeDtypeStruct((m, LANES), BF16)],
        scratch_shapes=[pltpu.VMEM((k, n), BF16)],
        compiler_params=_params(1), name="mla_down",
    )(h, w, gq.reshape(1, -1), gkv.reshape(1, -1), cos, sin)


def _mla_attn_kernel(qn_ref, qr_ref, kv_ref, kr_ref, o_ref,
                     s_ref, p_ref, m_ref, l_ref, acc_ref, *, tq, tk, nh, rg):
    i = pl.program_id(1)
    hd = NOPE_DIM
    m_ref[...] = jnp.full(m_ref.shape, NEG_BIG, F32)
    l_ref[...] = jnp.zeros_like(l_ref)
    acc_ref[...] = jnp.zeros_like(acc_ref)

    def step(kb, diag):
        rows = pl.ds(pl.multiple_of(kb * tk, tk), tk)
        kr = kr_ref[rows, :]
        for h in range(nh):
            q = jnp.concatenate([qn_ref[:, h * hd:(h + 1) * hd],
                                 qr_ref[:, h * hd:(h + 1) * hd]], axis=1)
            k = jnp.concatenate([kv_ref[rows, 2 * h * hd:(2 * h + 1) * hd], kr], axis=1)
            s = lax.dot_general(q, k, (((1,), (1,)), ((), ())),
                                preferred_element_type=F32)
            if diag is not None:
                qc = lax.broadcasted_iota(jnp.int32, (tq, tk), 0) // CHUNK
                kc = (lax.broadcasted_iota(jnp.int32, (tq, tk), 1) + diag * tk) // CHUNK
                s = jnp.where(kc <= qc, s, NEG_BIG)
            s_ref[h] = s
        nt = tk // LANES
        alphas = []
        for h in range(nh):
            m_tile = s_ref[h, :, 0:LANES]
            for c in range(1, nt):
                m_tile = jnp.maximum(m_tile, s_ref[h, :, c * LANES:(c + 1) * LANES])
            m_prev = m_ref[h]
            m_rep = jnp.maximum(m_prev, jnp.max(m_tile, axis=1, keepdims=True))
            m_ref[h] = m_rep
            alpha = jnp.exp2(m_prev - m_rep)
            alphas.append(alpha)
            for r in range(tq // rg):
                rs = slice(r * rg, (r + 1) * rg)
                part = None
                for c in range(nt):
                    cs = slice(c * LANES, (c + 1) * LANES)
                    p = jnp.exp2(s_ref[h, rs, cs] - m_rep[rs])
                    p_ref[h, rs, cs] = p.astype(BF16)
                    part = p if part is None else part + p
                l_ref[h, rs, :] = alpha[rs] * l_ref[h, rs, :] + part
        for h in range(nh):
            v = kv_ref[rows, (2 * h + 1) * hd:(2 * h + 2) * hd]
            acc_ref[h] = alphas[h] * acc_ref[h] + jnp.dot(p_ref[h], v, preferred_element_type=F32)

    def body(kp, c):
        for u in range(4):
            step(4 * kp + u, None)
        return c

    nd = tq // tk
    n_full = i * nd
    lax.fori_loop(0, n_full // 4, body, 0)
    rem = n_full % 4

    @pl.when(rem >= 2)
    def _():
        step(n_full - rem, None)
        step(n_full - rem + 1, None)

    @pl.when(rem % 2 == 1)
    def _():
        step(n_full - 1, None)

    for d in range(nd):
        step(n_full + d, d)
    for h in range(nh):
        l = jnp.sum(l_ref[h], axis=1, keepdims=True)
        o_ref[:, h * hd:(h + 1) * hd] = (acc_ref[h] / l).astype(o_ref.dtype)


def _mla_attn(qn, qr, kv, kr, tq=512, tk=512, nh=4, rg=32):
    s = qn.shape[0]
    tq, tk = min(tq, s), min(tk, s)
    assert CHUNK % rg == 0 and tk % CHUNK == 0 and tq % tk == 0
    return pl.pallas_call(
        functools.partial(_mla_attn_kernel, tq=tq, tk=tk, nh=nh, rg=rg),
        scratch_shapes=[pltpu.VMEM((nh, tq, tk), F32), pltpu.VMEM((nh, tq, tk), BF16),
                        pltpu.VMEM((nh, tq, LANES), F32), pltpu.VMEM((nh, tq, LANES), F32),
                        pltpu.VMEM((nh, tq, V_DIM), F32)],
        grid=(MLA_HEADS // nh, s // tq),
        in_specs=[pl.BlockSpec((tq, nh * NOPE_DIM), lambda g, i: (i, g)),
                  pl.BlockSpec((tq, nh * LANES), lambda g, i: (i, g)),
                  pl.BlockSpec((s, nh * (NOPE_DIM + V_DIM)), lambda g, i: (0, g)),
                  pl.BlockSpec((s, LANES), lambda g, i: (0, 0))],
        out_specs=pl.BlockSpec((tq, nh * V_DIM), lambda g, i: (i, g)),
        out_shape=jax.ShapeDtypeStruct((s, MLA_HEADS * V_DIM), BF16),
        compiler_params=_params(2), name="mla_attn",
    )(qn, qr, kv, kr)


def _sb_attn_kernel(q_ref, k_ref, v_ref, o_ref, carry_ref, acc_ref, live_ref, *, tq, nh):
    i = pl.program_id(1)
    hd = SB_HEAD_DIM
    row = lax.broadcasted_iota(jnp.int32, (tq, tq), 0)
    col = lax.broadcasted_iota(jnp.int32, (tq, tq), 1)
    upper = jnp.where(row > col, 1.0, 0.0).astype(BF16)
    carry_ref[...] = jnp.zeros_like(carry_ref)
    acc_ref[...] = jnp.zeros_like(acc_ref)

    def step(kb, diagonal):
        rows = pl.ds(pl.multiple_of(kb * tq, tq), tq)
        heads = range(nh)
        hcols = [slice(h * hd, (h + 1) * hd) for h in heads]
        z = [lax.dot_general(q_ref[:, hcols[h]], k_ref[rows, hcols[h]], (((1,), (1,)), ((), ())),
                             preferred_element_type=F32) for h in heads]
        sp = [jnp.maximum(z[h], 0.0) + jnp.log2(1.0 + jnp.exp2(-jnp.abs(z[h]))) for h in heads]
        lk = [jnp.where(col < row, -sp[h], 0.0) if diagonal else -sp[h] for h in heads]
        later = []
        for h in heads:
            lk_hi = lk[h].astype(BF16)
            lk_lo = (lk[h] - lk_hi.astype(F32)).astype(BF16)
            later.append(jnp.dot(lk_hi, upper, preferred_element_type=F32)
                         + jnp.dot(lk_lo, upper, preferred_element_type=F32))
        top = None
        for h in heads:
            carry = carry_ref[h]
            a = jnp.exp2(z[h] - sp[h] + later[h] + jnp.tile(carry, (1, tq // LANES)))
            if diagonal:
                a = jnp.where(col < row, a, 0.0)
            acc_ref[:, hcols[h]] += jnp.dot(a.astype(BF16), v_ref[rows, hcols[h]],
                                            preferred_element_type=F32)
            carry = carry + jnp.sum(lk[h], axis=1, keepdims=True)
            carry_ref[h] = carry
            top = carry if top is None else jnp.maximum(top, carry)
        return jnp.max(top) > -SB_STOP * LOG2E

    def cond(state):
        kb, live = state
        return jnp.logical_and(kb >= 0, live)

    def body(state):
        kb, _ = state
        return kb - 1, step(kb, False)

    @pl.when(i == 0)
    def _():
        step(0, True)
        live_ref[0] = 0

    @pl.when(i > 0)
    def _():
        step(i, True)
        live_ref[0] = step(i - 1, False).astype(jnp.int32)

    lax.while_loop(cond, body, (i - 2, live_ref[0] > 0))
    o_ref[...] = acc_ref[...].astype(o_ref.dtype)


def _sb_attn(qkv, tq=256, nh=4):
    s = qkv.shape[0]
    tq = min(tq, s)
    ng = SB_HEADS // nh
    w = nh * SB_HEAD_DIM
    return pl.pallas_call(
        functools.partial(_sb_attn_kernel, tq=tq, nh=nh),
        grid=(ng, s // tq),
        in_specs=[pl.BlockSpec((tq, w), lambda g, i: (i, g)),
                  pl.BlockSpec((s, w), lambda g, i: (0, ng + g)),
                  pl.BlockSpec((s, w), lambda g, i: (0, 2 * ng + g))],
        out_specs=pl.BlockSpec((tq, w), lambda g, i: (i, g)),
        out_shape=jax.ShapeDtypeStruct((s, SB_HEADS * SB_HEAD_DIM), BF16),
        scratch_shapes=[pltpu.VMEM((nh, tq, LANES), F32), pltpu.VMEM((tq, w), F32),
                        pltpu.SMEM((1,), jnp.int32)],
        compiler_params=_params(2), name="sb_attn",
    )(qkv, qkv, qkv)


def _proj_res_norm_kernel(a_ref, w_ref, res_ref, g_ref, x_ref, h_ref, wb_ref):
    @pl.when(pl.program_id(0) == 0)
    def _():
        _cast_rows(wb_ref, w_ref, 256)

    x = res_ref[...] + jnp.dot(a_ref[...], wb_ref[...], preferred_element_type=F32)
    x_ref[...] = x
    h_ref[...] = _rms(x, g_ref[...]).astype(BF16)


def _proj_res_norm(a, w, layer, res, g, tm=512):
    m, k = a.shape
    n = w.shape[2]
    tm = min(tm, m)
    row = lambda i: (i, 0)
    fixed = lambda i: (0, 0)
    return pl.pallas_call(
        _proj_res_norm_kernel,
        grid=(m // tm,),
        in_specs=[pl.BlockSpec((tm, k), row),
                  pl.BlockSpec((None, k, n), lambda i: (layer, 0, 0),
                               pipeline_mode=pl.Buffered(1)),
                  pl.BlockSpec((tm, n), row),
                  pl.BlockSpec((1, n), fixed)],
        out_specs=[pl.BlockSpec((tm, n), row), pl.BlockSpec((tm, n), row)],
        out_shape=[jax.ShapeDtypeStruct((m, n), F32),
                   jax.ShapeDtypeStruct((m, n), BF16)],
        scratch_shapes=[pltpu.VMEM((k, n), BF16)],
        compiler_params=_params(1), name="proj_res_norm",
    )(a, w, res, g.reshape(1, n))


def _mem_kv_kernel(mem_ref, g_ref, w_ref, o_ref):
    hm = _rms(mem_ref[...], g_ref[0]).astype(BF16)
    half = w_ref.shape[2] // 2
    for c in range(2):
        w = w_ref[0, :, c * half:(c + 1) * half].astype(BF16)
        o_ref[0, :, c * half:(c + 1) * half] = jnp.dot(
            hm, w, preferred_element_type=F32).astype(BF16)


def _mem_kv(mem, g, w):
    depth, d, n = w.shape
    nm = mem.shape[0]
    return pl.pallas_call(
        _mem_kv_kernel,
        grid=(depth,),
        in_specs=[pl.BlockSpec((nm, d), lambda l: (0, 0)),
                  pl.BlockSpec((1, 1, d), lambda l: (l, 0, 0)),
                  pl.BlockSpec((1, d, n), lambda l: (l, 0, 0))],
        out_specs=pl.BlockSpec((1, nm, n), lambda l: (l, 0, 0)),
        out_shape=jax.ShapeDtypeStruct((depth, nm, n), BF16),
        compiler_params=_params(1), name="mem_kv",
    )(mem, g.reshape(depth, 1, d), w)


def _mem_attn_kernel(h_ref, x_ref, wq_ref, kv_ref, wo_ref, g_ref,
                     xo_ref, ho_ref, wqb_ref, wob_ref):
    @pl.when(pl.program_id(0) == 0)
    def _():
        _cast_rows(wqb_ref, wq_ref, 256)
        _cast_rows(wob_ref, wo_ref, 256)

    hd = MEM_HEAD_DIM
    nk = MEM_HEADS * hd
    nm = kv_ref.shape[1]
    heads = range(MEM_HEADS)
    scale = LOG2E / math.sqrt(hd)
    q = (jnp.dot(h_ref[...], wqb_ref[...], preferred_element_type=F32) * scale).astype(BF16)
    s = [lax.dot_general(q[:, h * hd:(h + 1) * hd], kv_ref[0, :, h * hd:(h + 1) * hd],
                         (((1,), (1,)), ((), ())), preferred_element_type=F32)
         for h in heads]
    p = [jnp.exp2(s[h] - jnp.max(s[h], axis=1, keepdims=True)).astype(BF16) for h in heads]
    ones = jnp.ones((nm, hd), BF16)
    o = []
    for h in heads:
        v1 = jnp.concatenate([kv_ref[0, :, nk + h * hd:nk + (h + 1) * hd], ones], axis=1)
        ol = jnp.dot(p[h], v1, preferred_element_type=F32)
        o.append((ol[:, :hd] / ol[:, hd:]).astype(BF16))
    x = x_ref[...] + jnp.dot(jnp.concatenate(o, axis=1), wob_ref[...], preferred_element_type=F32)
    xo_ref[...] = x
    ho_ref[...] = _rms(x, g_ref[...]).astype(BF16)


def _mem_attn(h, x, wq, kv_all, layer, wo, g, tm=512):
    m, d = h.shape
    nq = wq.shape[2]
    tm = min(tm, m)
    row = lambda i: (i, 0)
    fixed = lambda i: (0, 0)
    at_layer = lambda i: (layer, 0, 0)
    nm, nkv = kv_all.shape[1:]
    return pl.pallas_call(
        _mem_attn_kernel,
        grid=(m // tm,),
        in_specs=[pl.BlockSpec((tm, d), row),
                  pl.BlockSpec((tm, d), row),
                  pl.BlockSpec((None, d, nq), at_layer, pipeline_mode=pl.Buffered(1)),
                  pl.BlockSpec((1, nm, nkv), at_layer),
                  pl.BlockSpec((None, nq, d), at_layer, pipeline_mode=pl.Buffered(1)),
                  pl.BlockSpec((1, d), fixed)],
        out_specs=[pl.BlockSpec((tm, d), row), pl.BlockSpec((tm, d), row)],
        out_shape=[jax.ShapeDtypeStruct((m, d), F32),
                   jax.ShapeDtypeStruct((m, d), BF16)],
        scratch_shapes=[pltpu.VMEM((d, nq), BF16), pltpu.VMEM((nq, d), BF16)],
        compiler_params=_params(1), name="mem_attn",
    )(h, x, wq, kv_all, wo, g.reshape(1, d))


def _conv3(u_ref, r0, nr, cols, cw, cb):
    a = SUBLANES + r0
    return (cb + cw[2:3, :] * u_ref[a:a + nr, cols] + cw[1:2, :] * u_ref[a - 1:a - 1 + nr, cols]
            + cw[0:1, :] * u_ref[a - 2:a - 2 + nr, cols])


def _silu_gate(gate, up):
    return gate * (1.0 / (1.0 + jnp.exp(-gate))) * up


def _ffn_in_kernel(h_ref, wg_ref, wu_ref, cwg_ref, cwu_ref, cbg_ref, cbu_ref,
                   o_ref, wgb_ref, wub_ref, raw_ref, *, tn, tc, ni, nt):
    hs = SUBLANES
    tm = h_ref.shape[0]
    t = pl.program_id(0)
    slot = t % 2
    i_cur = jnp.minimum(t, nt - 1) % ni
    cur = (raw_ref.at[slot, 0], raw_ref.at[slot, 1])
    prev = (raw_ref.at[1 - slot, 0], raw_ref.at[1 - slot, 1])

    @pl.when(t == 0)
    def _():
        for r in prev:
            r[...] = jnp.zeros(r.shape, F32)

    @pl.when(jnp.logical_and(i_cur == 0, t < nt))
    def _():
        _cast_rows(wgb_ref, wg_ref, 256)
        _cast_rows(wub_ref, wu_ref, 256)
        for r in cur:
            r[:hs, :] = jnp.zeros((hs, tn), F32)

    @pl.when(i_cur > 0)
    def _():
        for r, p in zip(cur, prev):
            r[:hs, :] = p[tm:tm + hs, :]

    def epilogue(r0, nr):
        tok = jnp.zeros((hs, LANES), F32)
        for c in range(tn // tc):
            cols = slice(c * tc, (c + 1) * tc)
            gate = _conv3(prev[0], r0, nr, cols, cwg_ref[:, cols], cbg_ref[:, cols])
            up = _conv3(prev[1], r0, nr, cols, cwu_ref[:, cols], cbu_ref[:, cols])
            g = _silu_gate(gate, up)
            o_ref[r0:r0 + nr, cols] = g.astype(o_ref.dtype)
            part = jnp.sum(g.reshape(nr // hs, hs, tc), axis=0)
            for l in range(tc // LANES):
                tok = tok + part[:, l * LANES:(l + 1) * LANES]
        return tok

    h = h_ref[...]
    q = tm // 4
    tok = epilogue(0, q)
    acc = jnp.dot(h, wgb_ref[...], preferred_element_type=F32)
    cur[0][hs:, :] = acc
    zero = (pltpu.bitcast(tok, jnp.uint32) >> 16) >> 16
    cur[0][hs:2 * hs, :LANES] = jnp.where(zero == 0, acc[:hs, :LANES], acc[hs:2 * hs, :LANES])
    epilogue(q, tm - q)
    cur[1][hs:, :] = jnp.dot(h, wub_ref[...], preferred_element_type=F32)


def _ffn_in(h, w_in, conv_w, conv_b, layer, tm=1024, tn=512, tc=256):
    m, k = h.shape
    dff = w_in.shape[2] // 2
    tm = min(tm, m)
    nj, ni = dff // tn, m // tm
    nt = nj * ni
    conv_b = conv_b.reshape(conv_b.shape[0], 1, -1)
    mm_tile = lambda t: jnp.minimum(t, nt - 1)
    ep_tile = lambda t: jnp.maximum(t - 1, 0)
    gate_w = lambda t: (layer, 0, mm_tile(t) // ni)
    up_w = lambda t: (layer, 0, nj + mm_tile(t) // ni)
    gate_c = lambda t: (layer, 0, ep_tile(t) // ni)
    up_c = lambda t: (layer, 0, nj + ep_tile(t) // ni)
    return pl.pallas_call(
        functools.partial(_ffn_in_kernel, tn=tn, tc=tc, ni=ni, nt=nt),
        grid=(nt + 1,),
        in_specs=[pl.BlockSpec((tm, k), lambda t: (mm_tile(t) % ni, 0)),
                  pl.BlockSpec((None, k, tn), gate_w),
                  pl.BlockSpec((None, k, tn), up_w),
                  pl.BlockSpec((None, CONV_W, tn), gate_c),
                  pl.BlockSpec((None, CONV_W, tn), up_c),
                  pl.BlockSpec((None, 1, tn), gate_c),
                  pl.BlockSpec((None, 1, tn), up_c)],
        out_specs=pl.BlockSpec((tm, tn), lambda t: (ep_tile(t) % ni, ep_tile(t) // ni)),
        out_shape=jax.ShapeDtypeStruct((m, dff), BF16),
        scratch_shapes=[pltpu.VMEM((k, tn), BF16), pltpu.VMEM((k, tn), BF16),
                        pltpu.VMEM((2, 2, SUBLANES + tm, tn), F32)],
        compiler_params=_params(1), name="ffn_in",
    )(h, w_in, w_in, conv_w, conv_w, conv_b, conv_b)


def _rope_slots(t):
    half = ROPE_DIM // 2
    z = jnp.zeros(t.shape[:-1] + (LANES // 2 - half,), t.dtype)
    return jnp.concatenate([t[..., :half], z, t[..., half:], z], axis=-1)


def _rope_tables(positions):
    inv_freq = ROPE_THETA ** (-jnp.arange(0, ROPE_DIM, 2, dtype=F32) / ROPE_DIM)
    ang = positions.astype(F32)[:, None] * inv_freq
    cos, sin = jnp.cos(ang), jnp.sin(ang)
    return (_rope_slots(jnp.concatenate([cos, cos], axis=-1)),
            _rope_slots(jnp.concatenate([-sin, sin], axis=-1)))


def kernel(x, mem, positions, norm_mix, norm_mem_q, norm_mem_kv, norm_ffn, norm_final, mla_w_down, mla_q_norm, mla_w_uq, mla_kv_norm, mla_w_ukv, mla_w_o, sb_w_qkv, sb_w_o, mem_w_q, mem_w_kv, mem_w_o, ffn_w_in, ffn_conv_w, ffn_conv_b, ffn_w_out):
    assert x.shape[0] == 1, "batch is fixed to 1"
    xs = x[0]
    cos, sin = _rope_tables(positions[0])
    mem_kv = _mem_kv(mem[0], norm_mem_kv, mem_w_kv)
    mla_scale = LOG2E / math.sqrt(NOPE_DIM + ROPE_DIM)
    sb_scale = LOG2E / math.sqrt(SB_HEAD_DIM)

    h = _rmsnorm(xs, norm_mix[0], BF16)
    for i in range(DEPTH):
        j = i // N_MIXERS
        if i % N_MIXERS == 0:
            nlat = Q_LORA + KV_LORA
            w_down = jnp.concatenate(
                [mla_w_down[j][:, :nlat], _rope_slots(mla_w_down[j][:, nlat:])], axis=1)
            c_q, c_kv, k_rope = _mla_down(h, w_down, mla_q_norm[j], mla_kv_norm[j], cos, sin)
            w_uq = mla_w_uq[j].reshape(Q_LORA, MLA_HEADS, NOPE_DIM + ROPE_DIM)
            w_qn = w_uq[:, :, :NOPE_DIM].reshape(Q_LORA, MLA_HEADS * NOPE_DIM)
            w_qr = _rope_slots(w_uq[:, :, NOPE_DIM:]).reshape(Q_LORA, MLA_HEADS * LANES)
            qn = _mm(c_q, w_qn[None], 0, tm=2048, tn=1024, out_dtype=BF16, mode="scale",
                     scale=mla_scale)
            qr = _mm(c_q, w_qr[None], 0, tm=2048, tn=1024, out_dtype=BF16, mode="rope",
                     scale=mla_scale, cos=cos, sin=sin)
            kv = _mm(c_kv, mla_w_ukv, j, tm=2048, tn=1024, out_dtype=BF16)
            o = _mla_attn(qn, qr, kv, k_rope)
            w_o = mla_w_o
        else:
            qkv = _mm(h, sb_w_qkv, j, tm=1024, tn=1024, out_dtype=BF16, mode="scale_cols",
                      scale=sb_scale, scale_cols=SB_HEADS * SB_HEAD_DIM)
            o = _sb_attn(qkv)
            w_o = sb_w_o
        xs, hq = _proj_res_norm(o, w_o, j, xs, norm_mem_q[i])
        xs, hf = _mem_attn(hq, xs, mem_w_q, mem_kv, i, mem_w_o, norm_ffn[i])
        g = _ffn_in(hf, ffn_w_in, ffn_conv_w, ffn_conv_b, i)
        xs = _mm(g, ffn_w_out, i, tm=1024, tn=512, out_dtype=F32, mode="residual", res=xs,
                 single_buffer_w=True)
        last = i == DEPTH - 1
        h = _rmsnorm(xs, norm_final if last else norm_mix[i + 1], F32 if last else BF16)
    return h[None]
```
